```python
import math
import jax, jax.numpy as jnp
from jax import lax
import numpy as np

D_MODEL = 1024
BATCH = 8
SEQ = 8192
DEPTH = 1

SSD_EXPAND = 2
D_INNER = SSD_EXPAND * D_MODEL
SSD_HEAD_DIM = 64
SSD_HEADS = D_INNER // SSD_HEAD_DIM
SSD_GROUPS = 4
SSD_HEADS_PER_GROUP = SSD_HEADS // SSD_GROUPS
D_STATE = 128
D_CONV = 5
SSD_CHUNK = 128
CONV_DIM = D_INNER + 2 * SSD_GROUPS * D_STATE
NORM_EPS = 1e-5

ATTN_HEAD_DIM = 64
DIL_PATTERNS = ((128, 1), (512, 4), (2048, 16))
N_PATTERNS = len(DIL_PATTERNS)
HEADS_PER_PATTERN = 4
ATTN_HEADS = N_PATTERNS * HEADS_PER_PATTERN
ATTN_WIDTH = ATTN_HEADS * ATTN_HEAD_DIM
ATTN_OUT_WIDTH = HEADS_PER_PATTERN * ATTN_HEAD_DIM

D_FF = 4 * D_MODEL
N_BRANCHES = 2
IN_SPLITS = (D_INNER, CONV_DIM, SSD_HEADS, SSD_HEADS, ATTN_WIDTH, ATTN_WIDTH, ATTN_WIDTH, N_BRANCHES * D_MODEL)
IN_COLS = sum(IN_SPLITS)

kernel_name = "hybrid_ssd_dilated_attn_gated_deepnorm"


def layer_norm(x, g, b):
    xf = x.astype(jnp.float32)
    mu = jnp.mean(xf, axis=-1, keepdims=True)
    var = jnp.mean(jnp.square(xf - mu), axis=-1, keepdims=True)
    return ((xf - mu) * lax.rsqrt(var + NORM_EPS) * g.astype(jnp.float32) + b.astype(jnp.float32)).astype(x.dtype)


def centred_depthwise_conv(u, w, b):
    pad = D_CONV // 2
    out = lax.conv_general_dilated(u, w[:, None, :].astype(u.dtype), window_strides=(1,), padding=((pad, pad),),
                                   dimension_numbers=('NWC', 'WIO', 'NWC'), feature_group_count=u.shape[-1])
    return out + b.astype(u.dtype)


def segsum_exp(a_cs):
    q = a_cs.shape[-1]
    diff = a_cs[..., :, None] - a_cs[..., None, :]
    mask = jnp.tril(jnp.ones((q, q), dtype=bool))
    return jnp.where(mask, jnp.exp(jnp.where(mask, diff, 0.0)), 0.0)


def ssd_chunked(xh, dt, a_coef, bm, cm):
    bsz, s = xh.shape[:2]
    nc = s // SSD_CHUNK
    g, r, p = SSD_GROUPS, SSD_HEADS_PER_GROUP, SSD_HEAD_DIM
    xc = (xh * dt[..., None]).reshape(bsz, nc, SSD_CHUNK, g, r, p)
    a = (dt * a_coef).reshape(bsz, nc, SSD_CHUNK, g, r).transpose(0, 3, 4, 1, 2)
    bc = bm.reshape(bsz, nc, SSD_CHUNK, g, D_STATE)
    cc = cm.reshape(bsz, nc, SSD_CHUNK, g, D_STATE)
    a_cs = jnp.cumsum(a, axis=-1)
    lmat = segsum_exp(a_cs)
    cb = jnp.einsum('bclgn,bcsgn->bgcls', cc, bc)
    y_diag = jnp.einsum('bgcls,bgrcls,bcsgrp->bclgrp', cb, lmat, xc)
    decay_states = jnp.exp(a_cs[..., -1:] - a_cs)
    states = jnp.einsum('bclgn,bgrcl,bclgrp->bcgrpn', bc, decay_states, xc)
    chunk_decay = jnp.exp(a_cs[..., -1])

    def step(h, inp):
        dec, st = inp
        return dec[..., None, None] * h + st, h

    h0 = jnp.zeros_like(states[:, 0])
    _, prev = lax.scan(step, h0, (jnp.moveaxis(chunk_decay, -1, 0), jnp.moveaxis(states, 1, 0)))
    prev = jnp.moveaxis(prev, 0, 1)
    y_off = jnp.einsum('bclgn,bcgrpn,bgrcl->bclgrp', cc, prev, jnp.exp(a_cs))
    return (y_diag + y_off).reshape(bsz, s, g * r, p)


def ssd_branch(z, xbc, dt_f_raw, dt_b_raw, conv_w, conv_b, dt_bias_f, dt_bias_b, a_log_f, a_log_b, d_skip, ssd_norm_w):
    bsz, s = z.shape[:2]
    xbc = jax.nn.silu(centred_depthwise_conv(xbc, conv_w, conv_b)).astype(jnp.float32)
    xs, bm, cm = jnp.split(xbc, [D_INNER, D_INNER + SSD_GROUPS * D_STATE], axis=-1)
    xh = xs.reshape(bsz, s, SSD_HEADS, SSD_HEAD_DIM)
    bm = bm.reshape(bsz, s, SSD_GROUPS, D_STATE)
    cm = cm.reshape(bsz, s, SSD_GROUPS, D_STATE)
    dt_f = jax.nn.softplus(dt_f_raw.astype(jnp.float32) + dt_bias_f.astype(jnp.float32))
    dt_b = jax.nn.softplus(dt_b_raw.astype(jnp.float32) + dt_bias_b.astype(jnp.float32))
    a_f = -jnp.exp(a_log_f.astype(jnp.float32))
    a_b = -jnp.exp(a_log_b.astype(jnp.float32))
    y_f = ssd_chunked(xh, dt_f, a_f, bm, cm)
    flip = lambda t: jnp.flip(t, axis=1)
    y_b = flip(ssd_chunked(flip(xh), flip(dt_b), a_b, flip(bm), flip(cm)))
    y = y_f + y_b + d_skip.astype(jnp.float32)[:, None] * xh
    y = y.reshape(bsz, s, D_INNER) * jax.nn.silu(z.astype(jnp.float32))
    yg = y.reshape(bsz, s, SSD_GROUPS, D_INNER // SSD_GROUPS)
    yg = yg * lax.rsqrt(jnp.mean(jnp.square(yg), axis=-1, keepdims=True) + NORM_EPS)
    return (yg.reshape(bsz, s, D_INNER) * ssd_norm_w.astype(jnp.float32)).astype(z.dtype)


def dilated_window_attention(q, k, v, slopes, dilation, half):
    bsz, s, h, e = q.shape
    seq_l = s // dilation
    blk = half
    nb = -(-seq_l // blk)
    lp = nb * blk

    def to_strided(a):
        return a.astype(jnp.float32).reshape(bsz, seq_l, dilation, h, e).transpose(0, 2, 3, 1, 4)

    qs = jnp.pad(to_strided(q), ((0, 0), (0, 0), (0, 0), (0, lp - seq_l), (0, 0))).reshape(bsz, dilation, h, nb, blk, e)

    def windows(a):
        a = jnp.pad(to_strided(a), ((0, 0), (0, 0), (0, 0), (blk, blk + lp - seq_l), (0, 0)))
        a = a.reshape(bsz, dilation, h, nb + 2, blk, e)
        return jnp.concatenate([a[:, :, :, :-2], a[:, :, :, 1:-1], a[:, :, :, 2:]], axis=4)

    ks, vs = windows(k), windows(v)
    qpos = jnp.arange(nb)[:, None] * blk + jnp.arange(blk)[None, :]
    kpos = jnp.arange(nb)[:, None] * blk - blk + jnp.arange(3 * blk)[None, :]
    rel = kpos[:, None, :] - qpos[:, :, None]
    valid = (jnp.abs(rel) <= half) & (kpos[:, None, :] >= 0) & (kpos[:, None, :] < seq_l)
    dist = (jnp.abs(rel) * dilation).astype(jnp.float32)
    scores = jnp.einsum('bdhine,bdhime->bdhinm', qs, ks) * (1.0 / math.sqrt(e))
    scores = scores - slopes.astype(jnp.float32)[:, None, None, None] * dist
    scores = jnp.where(valid, scores, -jnp.inf)
    m = jnp.max(scores, axis=-1, keepdims=True)
    p = jnp.exp(scores - m)
    den = jnp.sum(p, axis=-1, keepdims=True)
    o = jnp.einsum('bdhinm,bdhime->bdhine', p, vs) / den
    lse = (m + jnp.log(den))[..., 0]

    def from_strided(a):
        a = a.reshape(bsz, dilation, h, lp, *a.shape[5:])[:, :, :, :seq_l]
        a = jnp.moveaxis(a, 3, 1)
        return a.reshape(bsz, s, h, *a.shape[4:])

    return from_strided(o), from_strided(lse)


def attention_branch(q, k, v):
    bsz, s = q.shape[:2]
    q = q.reshape(bsz, s, ATTN_HEADS, ATTN_HEAD_DIM)
    k = k.reshape(bsz, s, ATTN_HEADS, ATTN_HEAD_DIM)
    v = v.reshape(bsz, s, ATTN_HEADS, ATTN_HEAD_DIM)
    slopes = jnp.asarray(2.0 ** (-8.0 * np.arange(1, ATTN_HEADS + 1) / ATTN_HEADS), dtype=jnp.float32)
    outs, lses = [], []
    for gi, (window, dilation) in enumerate(DIL_PATTERNS):
        hs = slice(gi * HEADS_PER_PATTERN, (gi + 1) * HEADS_PER_PATTERN)
        o, l = dilated_window_attention(q[:, :, hs], k[:, :, hs], v[:, :, hs], slopes[hs], dilation, window // (2 * dilation))
        outs.append(o)
        lses.append(l)
    o = jnp.stack(outs, axis=0)
    lse = jnp.stack(lses, axis=0)
    w = jax.nn.softmax(lse, axis=0)
    y = jnp.sum(w[..., None] * o, axis=0)
    return y.reshape(bsz, s, ATTN_OUT_WIDTH).astype(q.dtype)


def setup_inputs(seed: int = 0) -> dict:
    key = jax.random.key(seed)
    ks = jax.random.split(key, 24)
    beta = (8.0 * DEPTH) ** -0.25
    nrm = lambda k, shape: jax.random.normal(k, shape, dtype=jnp.float32)

    def dt_bias(k):
        dt = jnp.exp(jax.random.uniform(k, (SSD_HEADS,), minval=math.log(1e-3), maxval=math.log(1e-1)))
        return dt + jnp.log(-jnp.expm1(-dt))

    return {
        "x": nrm(ks[0], (BATCH, SEQ, D_MODEL)),
        "w_in": nrm(ks[1], (D_MODEL, IN_COLS)) * D_MODEL ** -0.5,
        "b_gate": 0.02 * nrm(ks[2], (N_BRANCHES * D_MODEL,)),
        "conv_w": nrm(ks[3], (D_CONV, CONV_DIM)) * D_CONV ** -0.5,
        "conv_b": 0.02 * nrm(ks[4], (CONV_DIM,)),
        "dt_bias_f": dt_bias(ks[5]),
        "dt_bias_b": dt_bias(ks[6]),
        "a_log_f": jnp.log(jax.random.uniform(ks[7], (SSD_HEADS,), minval=1.0, maxval=16.0)),
        "a_log_b": jnp.log(jax.random.uniform(ks[8], (SSD_HEADS,), minval=1.0, maxval=16.0)),
        "d_skip": 1.0 + 0.1 * nrm(ks[9], (SSD_HEADS,)),
        "ssd_norm_w": 1.0 + 0.1 * nrm(ks[10], (D_INNER,)),
        "w_proj_ssd": nrm(ks[11], (D_INNER, D_MODEL)) * D_INNER ** -0.5 * beta,
        "w_proj_attn": nrm(ks[12], (ATTN_OUT_WIDTH, D_MODEL)) * ATTN_OUT_WIDTH ** -0.5 * beta,
        "w_out": nrm(ks[13], (D_MODEL, D_MODEL)) * D_MODEL ** -0.5 * beta,
        "ln1_g": 1.0 + 0.1 * nrm(ks[14], (D_MODEL,)),
        "ln1_b": 0.02 * nrm(ks[15], (D_MODEL,)),
        "w_up": nrm(ks[16], (D_MODEL, D_FF)) * D_MODEL ** -0.5 * beta,
        "w_down": nrm(ks[17], (D_FF, D_MODEL)) * D_FF ** -0.5 * beta,
        "ln2_g": 1.0 + 0.1 * nrm(ks[18], (D_MODEL,)),
        "ln2_b": 0.02 * nrm(ks[19], (D_MODEL,)),
    }


def reference(x, w_in, b_gate, conv_w, conv_b, dt_bias_f, dt_bias_b, a_log_f, a_log_b, d_skip, ssd_norm_w,
              w_proj_ssd, w_proj_attn, w_out, ln1_g, ln1_b, w_up, w_down, ln2_g, ln2_b):
    alpha = (2.0 * DEPTH) ** 0.25
    h = x
    for _ in range(DEPTH):
        bsz, s = h.shape[:2]
        u = h @ w_in
        z, xbc, dt_f_raw, dt_b_raw, q, k, v, gate_logits = jnp.split(u, list(np.cumsum(IN_SPLITS)[:-1]), axis=-1)
        y_ssd = ssd_branch(z, xbc, dt_f_raw, dt_b_raw, conv_w, conv_b, dt_bias_f, dt_bias_b,
                           a_log_f, a_log_b, d_skip, ssd_norm_w) @ w_proj_ssd
        y_att = attention_branch(q, k, v) @ w_proj_attn
        gates = jax.nn.sigmoid(gate_logits + b_gate).reshape(bsz, s, N_BRANCHES, D_MODEL)
        mix = (gates[:, :, 0] * y_ssd + gates[:, :, 1] * y_att) @ w_out
        h = layer_norm(alpha * h + mix, ln1_g, ln1_b)
        f = jnp.square(jax.nn.relu(h @ w_up)) @ w_down
        h = layer_norm(alpha * h + f, ln2_g, ln2_b)
    return h
```

```python
import functools
import math

import jax
import jax.numpy as jnp
import numpy as np
from jax import lax
from jax.experimental import pallas as pl
from jax.experimental.pallas import tpu as pltpu

D_MODEL = 1024
D_INNER = 2048
SSD_HEAD_DIM = 64
SSD_GROUPS = 4
SSD_HEADS_PER_GROUP = 8
D_STATE = 128
D_CONV = 5
SSD_CHUNK = 128
GROUP_X = SSD_HEADS_PER_GROUP * SSD_HEAD_DIM
GROUP_XBC = GROUP_X + 2 * D_STATE
CONV_DIM = D_INNER + 2 * SSD_GROUPS * D_STATE
NORM_EPS = 1e-5
ATTN_HEAD_DIM = 64
DILATIONS = (1, 4, 16)
ATTN_HALF = 64
HEADS_PER_PATTERN = 4
PATTERN_WIDTH = HEADS_PER_PATTERN * ATTN_HEAD_DIM
QKV_WIDTH = 3 * PATTERN_WIDTH
ATTN_WIDTH = 3 * PATTERN_WIDTH
D_FF = 4 * D_MODEL
DT_PER_GROUP = 2 * SSD_HEADS_PER_GROUP
DT_ROWS = SSD_GROUPS * DT_PER_GROUP
ALPHA = 2.0 ** 0.25
NEG_BIG = -1e30
LOG2E = 1.4426950408889634

V7X_VMEM_LIMIT = 56 * 1024 * 1024
LANES = 128
BF16_SUBLANES = 16

_OFF_Z = 0
_OFF_XBC = _OFF_Z + D_INNER
_OFF_QKV = _OFF_XBC + CONV_DIM
_OFF_GATE = _OFF_QKV + 3 * QKV_WIDTH
IN_COLS_PACKED = _OFF_GATE + 2 * D_MODEL


def _group_xbc(a):
    nb = SSD_GROUPS * D_STATE
    xs, bs, cs = a[..., :D_INNER], a[..., D_INNER:D_INNER + nb], a[..., D_INNER + nb:]
    parts = []
    for g in range(SSD_GROUPS):
        parts += [xs[..., g * GROUP_X:(g + 1) * GROUP_X], bs[..., g * D_STATE:(g + 1) * D_STATE],
                  cs[..., g * D_STATE:(g + 1) * D_STATE]]
    return jnp.concatenate(parts, axis=-1)


def _pack_w_in(w_in):
    splits = np.cumsum([D_INNER, CONV_DIM, 32, 32, ATTN_WIDTH, ATTN_WIDTH, ATTN_WIDTH])
    wz, wxbc, wdtf, wdtb, wq, wk, wv, wg = jnp.split(w_in, list(splits), axis=1)
    wxbc = _group_xbc(wxbc)
    wq = wq * (1.0 / math.sqrt(ATTN_HEAD_DIM))
    qkv = []
    for p in range(3):
        s = slice(p * PATTERN_WIDTH, (p + 1) * PATTERN_WIDTH)
        qkv += [wq[:, s], wk[:, s], wv[:, s]]
    dts = []
    for g in range(SSD_GROUPS):
        s = slice(g * SSD_HEADS_PER_GROUP, (g + 1) * SSD_HEADS_PER_GROUP)
        dts += [wdtf[:, s], wdtb[:, s]]
    packed = jnp.concatenate([wz, wxbc] + qkv + [wg], axis=1).astype(jnp.bfloat16)
    wdt_t = jnp.concatenate(dts, axis=1).T.astype(jnp.bfloat16)
    return packed, wdt_t


def _in_proj_kernel(x_ref, w_ref, wdt_t_ref, z_ref, xbc_ref, qkv1_ref, qkv2_ref, qkv3_ref, gate_ref,
                    dtt_ref, stage_ref, *, tm, nc):
    xb = x_ref[0].astype(jnp.bfloat16)

    def proj(off, width):
        return jnp.dot(xb, w_ref[:, off:off + width], preferred_element_type=jnp.float32)

    for c in range(0, D_INNER, nc):
        z_ref[0, :, c:c + nc] = proj(_OFF_Z + c, nc).astype(z_ref.dtype)
    for c in range(0, CONV_DIM, nc):
        xbc_ref[0, :, c:c + nc] = proj(_OFF_XBC + c, nc).astype(xbc_ref.dtype)
    for c in range(0, 2 * D_MODEL, nc):
        gate_ref[0, :, c:c + nc] = proj(_OFF_GATE + c, nc).astype(gate_ref.dtype)
    qkv1_ref[0] = proj(_OFF_QKV, QKV_WIDTH).astype(qkv1_ref.dtype)
    for out_ref, pat, dil in ((qkv2_ref, 1, DILATIONS[1]), (qkv3_ref, 2, DILATIONS[2])):
        res = proj(_OFF_QKV + pat * QKV_WIDTH, QKV_WIDTH)
        for c in range(QKV_WIDTH // LANES):
            stage_ref[c] = res[:, c * LANES:(c + 1) * LANES]
        for r in range(dil):
            for c in range(QKV_WIDTH // LANES):
                out_ref[0, r, :, c * LANES:(c + 1) * LANES] = (
                    stage_ref[c, pl.ds(r, tm // dil, stride=dil), :].astype(out_ref.dtype))
    dtt = lax.dot_general(wdt_t_ref[...], xb, (((1,), (1,)), ((), ())),
                          preferred_element_type=jnp.float32)
    for g in range(SSD_GROUPS):
        dtt_ref[0, g] = dtt[g * DT_PER_GROUP:(g + 1) * DT_PER_GROUP, :]


def _in_proj(x, w_packed, wdt_t, *, tm=512, nc=512):
    bsz, s, d = x.shape
    assert s % tm == 0 and tm % (BF16_SUBLANES * DILATIONS[2]) == 0
    act = jnp.bfloat16
    out_shape = (
        jax.ShapeDtypeStruct((bsz, s, D_INNER), act),
        jax.ShapeDtypeStruct((bsz, s, CONV_DIM), act),
        jax.ShapeDtypeStruct((bsz, s, QKV_WIDTH), act),
        jax.ShapeDtypeStruct((bsz, DILATIONS[1], s // DILATIONS[1], QKV_WIDTH), act),
        jax.ShapeDtypeStruct((bsz, DILATIONS[2], s // DILATIONS[2], QKV_WIDTH), act),
        jax.ShapeDtypeStruct((bsz, s, 2 * D_MODEL), act),
        jax.ShapeDtypeStruct((bsz, SSD_GROUPS, DT_PER_GROUP, s), jnp.float32),
    )
    tok = lambda width: pl.BlockSpec((1, tm, width), lambda b, i: (b, i, 0))
    resident = lambda shape: pl.BlockSpec(shape, lambda b, i: (0,) * len(shape),
                                          pipeline_mode=pl.Buffered(1))
    return pl.pallas_call(
        functools.partial(_in_proj_kernel, tm=tm, nc=nc),
        grid=(bsz, s // tm),
        in_specs=[tok(d), resident(w_packed.shape), resident(wdt_t.shape)],
        out_specs=(
            tok(D_INNER), tok(CONV_DIM), tok(QKV_WIDTH),
            pl.BlockSpec((1, DILATIONS[1], tm // DILATIONS[1], QKV_WIDTH), lambda b, i: (b, 0, i, 0)),
            pl.BlockSpec((1, DILATIONS[2], tm // DILATIONS[2], QKV_WIDTH), lambda b, i: (b, 0, i, 0)),
            tok(2 * D_MODEL),
            pl.BlockSpec((1, SSD_GROUPS, DT_PER_GROUP, tm), lambda b, i: (b, 0, 0, i)),
        ),
        out_shape=out_shape,
        scratch_shapes=[pltpu.VMEM((QKV_WIDTH // LANES, tm, LANES), jnp.float32)],
        compiler_params=pltpu.CompilerParams(
            dimension_semantics=("arbitrary", "arbitrary"), vmem_limit_bytes=V7X_VMEM_LIMIT),
        name="in_proj",
    )(x, w_packed, wdt_t)


def _sigmoid(v):
    return 0.5 * (1.0 + jnp.tanh(0.5 * v))


def _split3_bf16(a):
    hi = a.astype(jnp.bfloat16)
    r1 = a - hi.astype(jnp.float32)
    mid = r1.astype(jnp.bfloat16)
    lo = (r1 - mid.astype(jnp.float32)).astype(jnp.bfloat16)
    return hi, mid, lo


def _ssd_chunk(xq, bq, cq, dt_raw_t, dt_bias_c, a_log_c, h_ref, *, backward):
    q = SSD_CHUNK
    f32, bf16 = jnp.float32, jnp.bfloat16
    row = lax.broadcasted_iota(jnp.int32, (q, q), 0)
    col = lax.broadcasted_iota(jnp.int32, (q, q), 1)
    lower = row >= col
    upper = row <= col
    tri = (upper if not backward else lower).astype(f32).astype(bf16)
    mask = lower if not backward else upper
    end = 0 if backward else q - 1

    v = dt_raw_t + dt_bias_c
    dt = jnp.maximum(v, 0.0) + jnp.log1p(jnp.exp(-jnp.abs(v)))
    a2 = dt * (-jnp.exp(a_log_c) * LOG2E)
    parts = jnp.concatenate(_split3_bf16(a2), axis=0)
    cum3 = jnp.dot(parts, tri, preferred_element_type=f32)
    n = DT_PER_GROUP
    cum2 = cum3[0:n] + cum3[n:2 * n] + cum3[2 * n:3 * n]
    tot2 = cum2[:, end:end + 1]
    e_r = jnp.exp2(cum2)
    w_r = dt * jnp.exp2(tot2 - cum2)
    rowp = cum2 - jnp.log2(dt)
    stacked = jnp.concatenate([cum2, e_r, w_r, jnp.zeros((q - 3 * n, q), f32)], axis=0)
    cols = stacked.T

    cb = lax.dot_general(cq, bq, (((1,), (1,)), ((), ())), preferred_element_type=f32)
    h_prev = h_ref[...]
    ch = jnp.dot(cq, h_prev.astype(bf16), preferred_element_type=f32)
    lane = lax.broadcasted_iota(jnp.int32, (q, LANES), 1)
    first_half = lane < SSD_HEAD_DIM
    hoff = SSD_HEADS_PER_GROUP if backward else 0

    ys, xws, decays = [], [], []
    for j in range(SSD_HEADS_PER_GROUP // 2):
        xpair = xq[:, j * LANES:(j + 1) * LANES]
        ms = []
        for hh in (hoff + 2 * j, hoff + 2 * j + 1):
            diff = cols[:, hh:hh + 1] - rowp[hh:hh + 1, :]
            ms.append((cb * jnp.exp2(jnp.where(mask, diff, NEG_BIG))).astype(bf16))
        lhs = jnp.concatenate(ms, axis=1)
        zero = jnp.zeros_like(xpair)
        rhs = jnp.concatenate([jnp.where(first_half, xpair, zero),
                               jnp.where(first_half, zero, xpair)], axis=0)
        y_diag = jnp.dot(lhs, rhs, preferred_element_type=f32)
        c0 = hoff + 2 * j
        e_pair = jnp.where(first_half, cols[:, n + c0:n + c0 + 1], cols[:, n + c0 + 1:n + c0 + 2])
        w_pair = jnp.where(first_half, cols[:, 2 * n + c0:2 * n + c0 + 1],
                           cols[:, 2 * n + c0 + 1:2 * n + c0 + 2])
        ys.append(y_diag + e_pair * ch[:, j * LANES:(j + 1) * LANES])
        xws.append((xpair.astype(f32) * w_pair).astype(bf16))
        decays.append(e_pair[end:end + 1, :])
    xw = jnp.concatenate(xws, axis=1)
    s_new = lax.dot_general(bq, xw, (((0,), (0,)), ((), ())), preferred_element_type=f32)
    h_ref[...] = jnp.concatenate(decays, axis=1) * h_prev + s_new
    return jnp.concatenate(ys, axis=1)


def _ssd_kernel(xbc_ref, prev_ref, next_ref, dtt_ref, z_ref, convw_ref, convb_ref, dtb_ref, alog_ref,
                dskip_ref, normw_ref, y_ref, ext_ref, xc_ref, yf_ref, h_ref, *, lb, nb):
    ph = pl.program_id(2)
    i = pl.program_id(3)
    q = SSD_CHUNK
    nck = lb // q
    f32 = jnp.float32
    halo = BF16_SUBLANES
    pad = D_CONV // 2

    @pl.when(i == 0)
    def _():
        h_ref[...] = jnp.zeros_like(h_ref)

    @pl.when(ph == 0)
    def _forward():
        ext_ref[0:halo, :] = jnp.where(i > 0, prev_ref[0].astype(f32), 0.0)
        ext_ref[halo:halo + lb, :] = xbc_ref[0].astype(f32)
        ext_ref[halo + lb:2 * halo + lb, :] = jnp.where(i < nb - 1, next_ref[0].astype(f32), 0.0)
        base = pl.multiple_of(i * lb, lb)
        for r0 in range(0, lb, q):
            for c0 in range(0, GROUP_XBC, LANES):
                acc = jnp.broadcast_to(convb_ref[0, :, c0:c0 + LANES], (q, LANES))
                for k in range(D_CONV):
                    lo = halo - pad + k + r0
                    acc = acc + convw_ref[0, k:k + 1, c0:c0 + LANES] * ext_ref[lo:lo + q, c0:c0 + LANES]
                xc_ref[pl.ds(base + r0, q), c0:c0 + LANES] = (acc * _sigmoid(acc)).astype(xc_ref.dtype)

        def chunk(c, carry):
            r = pl.multiple_of(base + c * q, q)
            lc = pl.multiple_of(c * q, q)
            y = _ssd_chunk(xc_ref[pl.ds(r, q), 0:GROUP_X], xc_ref[pl.ds(r, q), GROUP_X:GROUP_X + D_STATE],
                           xc_ref[pl.ds(r, q), GROUP_X + D_STATE:GROUP_XBC],
                           dtt_ref[0, 0, :, pl.ds(lc, q)], dtb_ref[0], alog_ref[0], h_ref, backward=False)
            yf_ref[pl.ds(r, q), :] = y
            return carry

        lax.fori_loop(0, nck, chunk, 0)

    @pl.when(ph == 1)
    def _backward():
        base = pl.multiple_of((nb - 1 - i) * lb, lb)

        def chunk(cc, carry):
            c = nck - 1 - cc
            r = pl.multiple_of(base + c * q, q)
            lc = pl.multiple_of(c * q, q)
            xq = xc_ref[pl.ds(r, q), 0:GROUP_X]
            y = _ssd_chunk(xq, xc_ref[pl.ds(r, q), GROUP_X:GROUP_X + D_STATE],
                           xc_ref[pl.ds(r, q), GROUP_X + D_STATE:GROUP_XBC],
                           dtt_ref[0, 0, :, pl.ds(lc, q)], dtb_ref[0], alog_ref[0], h_ref, backward=True)
            y = y + yf_ref[pl.ds(r, q), :] + dskip_ref[0] * xq.astype(f32)
            zq = z_ref[0, pl.ds(lc, q), :].astype(f32)
            y = y * (zq * _sigmoid(zq))
            ms = jnp.mean(y * y, axis=-1, keepdims=True)
            y_ref[0, pl.ds(lc, q), :] = (y * lax.rsqrt(ms + NORM_EPS) * normw_ref[0]).astype(y_ref.dtype)
            return carry

        lax.fori_loop(0, nck, chunk, 0)


def _ssd(xbc, dtt, z, conv_w, conv_b, dt_bias_f, dt_bias_b, a_log_f, a_log_b, d_skip, ssd_norm_w, *, lb=512):
    bsz, s, _ = xbc.shape
    nb = s // lb
    assert s % lb == 0 and lb % SSD_CHUNK == 0
    halo = BF16_SUBLANES
    hb = lb // halo
    f32 = jnp.float32
    convw = _group_xbc(conv_w.astype(f32)).reshape(D_CONV, SSD_GROUPS, GROUP_XBC).transpose(1, 0, 2)
    convb = _group_xbc(conv_b.astype(f32)).reshape(SSD_GROUPS, 1, GROUP_XBC)
    per_group = lambda f, b: jnp.concatenate(
        [f.astype(f32).reshape(SSD_GROUPS, SSD_HEADS_PER_GROUP), b.astype(f32).reshape(SSD_GROUPS, SSD_HEADS_PER_GROUP)],
        axis=1).reshape(SSD_GROUPS, DT_PER_GROUP, 1)
    dtb = per_group(dt_bias_f, dt_bias_b)
    alog = per_group(a_log_f, a_log_b)
    dskip = jnp.repeat(d_skip.astype(f32), SSD_HEAD_DIM).reshape(SSD_GROUPS, 1, GROUP_X)
    normw = ssd_norm_w.astype(f32).reshape(SSD_GROUPS, 1, GROUP_X)

    fwd_blk = lambda ph, i: jnp.where(ph == 0, i, nb - 1)
    any_blk = lambda ph, i: jnp.where(ph == 0, i, nb - 1 - i)
    bwd_blk = lambda ph, i: jnp.where(ph == 0, nb - 1, nb - 1 - i)
    par = lambda shape: pl.BlockSpec(shape, lambda b, g, ph, i: (g, 0, 0))
    return pl.pallas_call(
        functools.partial(_ssd_kernel, lb=lb, nb=nb),
        grid=(bsz, SSD_GROUPS, 2, nb),
        in_specs=[
            pl.BlockSpec((1, lb, GROUP_XBC), lambda b, g, ph, i: (b, fwd_blk(ph, i), g)),
            pl.BlockSpec((1, halo, GROUP_XBC), lambda b, g, ph, i: (b, jnp.maximum(fwd_blk(ph, i) * hb - 1, 0), g)),
            pl.BlockSpec((1, halo, GROUP_XBC),
                         lambda b, g, ph, i: (b, jnp.minimum((fwd_blk(ph, i) + 1) * hb, nb * hb - 1), g)),
            pl.BlockSpec((1, 1, DT_PER_GROUP, lb), lambda b, g, ph, i: (b, g, 0, any_blk(ph, i))),
            pl.BlockSpec((1, lb, GROUP_X), lambda b, g, ph, i: (b, bwd_blk(ph, i), g)),
            par((1, D_CONV, GROUP_XBC)), par((1, 1, GROUP_XBC)),
            par((1, DT_PER_GROUP, 1)), par((1, DT_PER_GROUP, 1)),
            par((1, 1, GROUP_X)), par((1, 1, GROUP_X)),
        ],
        out_specs=pl.BlockSpec((1, lb, GROUP_X), lambda b, g, ph, i: (b, bwd_blk(ph, i), g)),
        out_shape=jax.ShapeDtypeStruct((bsz, s, D_INNER), jnp.bfloat16),
        scratch_shapes=[
            pltpu.VMEM((lb + 2 * halo, GROUP_XBC), f32),
            pltpu.VMEM((s, GROUP_XBC), jnp.bfloat16),
            pltpu.VMEM((s, GROUP_X), f32),
            pltpu.VMEM((D_STATE, GROUP_X), f32),
        ],
        compiler_params=pltpu.CompilerParams(
            dimension_semantics=("arbitrary",) * 4, vmem_limit_bytes=V7X_VMEM_LIMIT),
        name="ssd",
    )(xbc, xbc, xbc, dtt, z, convw, convb, dtb, alog, dskip, normw)


ATTN_QB = 128
ATTN_WIN = ATTN_QB + 2 * ATTN_HALF


def _attn_bias_table():
    heads = len(DILATIONS) * HEADS_PER_PATTERN
    slopes = 2.0 ** (-8.0 * np.arange(1, heads + 1) / heads)
    rel = (np.arange(ATTN_WIN)[None, :] - ATTN_HALF) - np.arange(ATTN_QB)[:, None]
    tabs = []
    for h in range(heads):
        dil = DILATIONS[h // HEADS_PER_PATTERN]
        tabs.append(np.where(np.abs(rel) <= ATTN_HALF, -slopes[h] * np.abs(rel) * dil, NEG_BIG))
    return jnp.asarray(np.stack(tabs), dtype=jnp.float32)


def _attn_kernel(bias_ref, q1_ref, p1_ref, n1_ref, q2_ref, p2_ref, n2_ref, q3_ref, p3_ref, n3_ref, y_ref,
                 kv1_ref, kv2_ref, kv3_ref, o_ref, lse_ref, *, ta, nt, seq):
    i = pl.program_id(1)
    f32, bf16 = jnp.float32, jnp.bfloat16
    kvw = 2 * PATTERN_WIDTH
    lane = lax.broadcasted_iota(jnp.int32, (ATTN_QB, LANES), 1)
    first_half = lane < ATTN_HEAD_DIM
    kcol = lax.broadcasted_iota(jnp.int32, (1, ATTN_WIN), 1)

    for pat, (dil, q_ref, prev_ref, next_ref, kv_ref) in enumerate((
            (DILATIONS[0], q1_ref, p1_ref, n1_ref, kv1_ref),
            (DILATIONS[1], q2_ref, p2_ref, n2_ref, kv2_ref),
            (DILATIONS[2], q3_ref, p3_ref, n3_ref, kv3_ref))):
        n = ta // dil
        seq_l = seq // dil
        kv_ref[:, 0:ATTN_HALF, :] = prev_ref[0, :, :, PATTERN_WIDTH:]
        kv_ref[:, ATTN_HALF:ATTN_HALF + n, :] = q_ref[0, :, :, PATTERN_WIDTH:]
        kv_ref[:, ATTN_HALF + n:2 * ATTN_HALF + n, :] = next_ref[0, :, :, PATTERN_WIDTH:]
        nqb = n // ATTN_QB

        def block(idx, carry, dil=dil, q_ref=q_ref, kv_ref=kv_ref, nqb=nqb, n=n, seq_l=seq_l, pat=pat):
            r = idx // nqb
            jb = idx % nqb
            row0 = pl.multiple_of(jb * ATTN_QB, ATTN_QB)
            kpos = i * n + row0 - ATTN_HALF + kcol
            pen = jnp.where((kpos >= 0) & (kpos < seq_l), 0.0, NEG_BIG).astype(f32)
            for pair in range(HEADS_PER_PATTERN // 2):
                cs = slice(pair * LANES, (pair + 1) * LANES)
                qp = q_ref[0, r, pl.ds(row0, ATTN_QB), cs]
                kp = kv_ref[r, pl.ds(row0, ATTN_WIN), cs]
                vp = kv_ref[r, pl.ds(row0, ATTN_WIN), pair * LANES + PATTERN_WIDTH:(pair + 1) * LANES + PATTERN_WIDTH]
                zero = jnp.zeros_like(qp)
                outs, lses = [], []
                for sub in range(2):
                    head = pat * HEADS_PER_PATTERN + 2 * pair + sub
                    qh = jnp.where(first_half, qp, zero) if sub == 0 else jnp.where(first_half, zero, qp)
                    sc = lax.dot_general(qh, kp, (((1,), (1,)), ((), ())), preferred_element_type=f32)
                    sc = sc + bias_ref[head] + pen
                    m = jnp.max(sc, axis=-1, keepdims=True)
                    pe = jnp.exp(sc - m)
                    den = jnp.sum(pe, axis=-1, keepdims=True)
                    outs.append(jnp.dot(pe.astype(bf16), vp, preferred_element_type=f32) / den)
                    lses.append(m + jnp.log(den))
                rows = pl.ds(r + row0 * dil, ATTN_QB, stride=dil)
                o_ref[pat, pair, rows, :] = jnp.where(first_half, outs[0], outs[1])
                lse_ref[pat, pair, rows, :] = jnp.where(first_half, lses[0], lses[1])
            return carry

        lax.fori_loop(0, dil * nqb, block, 0)

    for pair in range(HEADS_PER_PATTERN // 2):
        l0, l1, l2 = lse_ref[0, pair], lse_ref[1, pair], lse_ref[2, pair]
        m = jnp.maximum(jnp.maximum(l0, l1), l2)
        w0, w1, w2 = jnp.exp(l0 - m), jnp.exp(l1 - m), jnp.exp(l2 - m)
        y = (w0 * o_ref[0, pair] + w1 * o_ref[1, pair] + w2 * o_ref[2, pair]) / (w0 + w1 + w2)
        y_ref[0, :, pair * LANES:(pair + 1) * LANES] = y.astype(y_ref.dtype)


def _attention(qkv1, qkv2, qkv3, *, ta=2048):
    bsz, s, _ = qkv1.shape
    nt = s // ta
    assert s % ta == 0 and ta % (ATTN_QB * DILATIONS[2]) == 0
    qkv1 = qkv1.reshape(bsz, 1, s, QKV_WIDTH)
    hb = ATTN_HALF
    specs = []
    for dil in DILATIONS:
        n = ta // dil
        last = s // dil // hb - 1
        specs += [
            pl.BlockSpec((1, dil, n, QKV_WIDTH), lambda b, i: (b, 0, i, 0)),
            pl.BlockSpec((1, dil, hb, QKV_WIDTH), lambda b, i, n=n: (b, 0, jnp.maximum(i * (n // hb) - 1, 0), 0)),
            pl.BlockSpec((1, dil, hb, QKV_WIDTH),
                         lambda b, i, n=n, last=last: (b, 0, jnp.minimum((i + 1) * (n // hb), last), 0)),
        ]
    bias = _attn_bias_table()
    kvw = 2 * PATTERN_WIDTH
    return pl.pallas_call(
        functools.partial(_attn_kernel, ta=ta, nt=nt, seq=s),
        grid=(bsz, nt),
        in_specs=[pl.BlockSpec(bias.shape, lambda b, i: (0, 0, 0))] + specs,
        out_specs=pl.BlockSpec((1, ta, PATTERN_WIDTH), lambda b, i: (b, i, 0)),
        out_shape=jax.ShapeDtypeStruct((bsz, s, PATTERN_WIDTH), jnp.bfloat16),
        scratch_shapes=[pltpu.VMEM((dil, ta // dil + 2 * hb, kvw), jnp.bfloat16) for dil in DILATIONS] + [
            pltpu.VMEM((len(DILATIONS), PATTERN_WIDTH // LANES, ta, LANES), jnp.float32),
            pltpu.VMEM((len(DILATIONS), PATTERN_WIDTH // LANES, ta, LANES), jnp.float32),
        ],
        compiler_params=pltpu.CompilerParams(
            dimension_semantics=("arbitrary", "arbitrary"), vmem_limit_bytes=V7X_VMEM_LIMIT),
        name="dilated_attn",
    )(bias, qkv1, qkv1, qkv1, qkv2, qkv2, qkv2, qkv3, qkv3, qkv3)


def _layer_norm(v, g, b):
    mu = jnp.mean(v, axis=-1, keepdims=True)
    c = v - mu
    var = jnp.mean(c * c, axis=-1, keepdims=True)
    return c * lax.rsqrt(var + NORM_EPS) * g + b


def _merge_mlp_kernel(x_ref, yssd_ref, yatt_ref, gate_ref, bgate_ref, wps_ref, wpa_ref, wout_ref,
                      ln1g_ref, ln1b_ref, wup_ref, wdown_ref, ln2g_ref, ln2b_ref, o_ref, *, ffc):
    f32, bf16 = jnp.float32, jnp.bfloat16
    ys = jnp.dot(yssd_ref[0], wps_ref[...], preferred_element_type=f32)
    ya = jnp.dot(yatt_ref[0], wpa_ref[...], preferred_element_type=f32)
    g_ssd = _sigmoid(gate_ref[0, :, 0:D_MODEL].astype(f32) + bgate_ref[:, 0:D_MODEL])
    g_att = _sigmoid(gate_ref[0, :, D_MODEL:2 * D_MODEL].astype(f32) + bgate_ref[:, D_MODEL:2 * D_MODEL])
    mix = jnp.dot((g_ssd * ys + g_att * ya).astype(bf16), wout_ref[...], preferred_element_type=f32)
    h = _layer_norm(ALPHA * x_ref[0] + mix, ln1g_ref[...], ln1b_ref[...])
    hb = h.astype(bf16)
    f = jnp.zeros_like(h)
    for c in range(0, D_FF, ffc):
        up = jnp.maximum(jnp.dot(hb, wup_ref[:, c:c + ffc], preferred_element_type=f32), 0.0)
        f = f + jnp.dot((up * up).astype(bf16), wdown_ref[c:c + ffc, :], preferred_element_type=f32)
    o_ref[0] = _layer_norm(ALPHA * h + f, ln2g_ref[...], ln2b_ref[...]).astype(o_ref.dtype)


def _merge_mlp(x, y_ssd, y_att, gate, b_gate, w_proj_ssd, w_proj_attn, w_out, ln1_g, ln1_b, w_up, w_down,
               ln2_g, ln2_b, *, tm=512, ffc=1024):
    bsz, s, d = x.shape
    assert s % tm == 0 and D_FF % ffc == 0
    bf16, f32 = jnp.bfloat16, jnp.float32
    row = lambda v: v.astype(f32).reshape(1, -1)
    tok = lambda width: pl.BlockSpec((1, tm, width), lambda b, i: (b, i, 0))
    resident = lambda shape: pl.BlockSpec(shape, lambda b, i: (0,) * len(shape), pipeline_mode=pl.Buffered(1))
    operands = [
        (x, tok(d)), (y_ssd, tok(D_INNER)), (y_att, tok(PATTERN_WIDTH)), (gate, tok(2 * D_MODEL)),
        (row(b_gate), None), (w_proj_ssd.astype(bf16), None), (w_proj_attn.astype(bf16), None),
        (w_out.astype(bf16), None), (row(ln1_g), None), (row(ln1_b), None),
        (w_up.astype(bf16), None), (w_down.astype(bf16), None), (row(ln2_g), None), (row(ln2_b), None),
    ]
    args = [a for a, _ in operands]
    in_specs = [spec if spec is not None else resident(a.shape) for a, spec in operands]
    return pl.pallas_call(
        functools.partial(_merge_mlp_kernel, ffc=ffc),
        grid=(bsz, s // tm),
        in_specs=in_specs,
        out_specs=tok(d),
        out_shape=jax.ShapeDtypeStruct((bsz, s, d), x.dtype),
        compiler_params=pltpu.CompilerParams(
            dimension_semantics=("arbitrary", "arbitrary"), vmem_limit_bytes=V7X_VMEM_LIMIT),
        name="merge_mlp",
    )(*args)


def kernel(x, w_in, b_gate, conv_w, conv_b, dt_bias_f, dt_bias_b, a_log_f, a_log_b, d_skip, ssd_norm_w,
           w_proj_ssd, w_proj_attn, w_out, ln1_g, ln1_b, w_up, w_down, ln2_g, ln2_b):
    w_packed, wdt_t = _pack_w_in(w_in)
    z, xbc, qkv1, qkv2, qkv3, gate, dtt = _in_proj(x, w_packed, wdt_t)
    y_ssd = _ssd(xbc, dtt, z, conv_w, conv_b, dt_bias_f, dt_bias_b, a_log_f, a_log_b, d_skip, ssd_norm_w)
    y_att = _attention(qkv1, qkv2, qkv3)
    return _merge_mlp(x, y_ssd, y_att, gate, b_gate, w_proj_ssd, w_proj_attn, w_out, ln1_g, ln1_b,
                      w_up, w_down, ln2_g, ln2_b)
```

```python
import functools
import math

import jax
import jax.numpy as jnp
import numpy as np
from jax import lax
from jax.experimental import pallas as pl
from jax.experimental.pallas import tpu as pltpu

D_MODEL = 1024
D_INNER = 2048
SSD_HEAD_DIM = 64
SSD_GROUPS = 4
SSD_HEADS_PER_GROUP = 8
D_STATE = 128
D_CONV = 5
SSD_CHUNK = 128
GROUP_X = SSD_HEADS_PER_GROUP * SSD_HEAD_DIM
GROUP_XBC = GROUP_X + 2 * D_STATE
CONV_DIM = D_INNER + 2 * SSD_GROUPS * D_STATE
NORM_EPS = 1e-5
ATTN_HEAD_DIM = 64
DILATIONS = (1, 4, 16)
ATTN_HALF = 64
HEADS_PER_PATTERN = 4
PATTERN_WIDTH = HEADS_PER_PATTERN * ATTN_HEAD_DIM
QKV_WIDTH = 3 * PATTERN_WIDTH
ATTN_WIDTH = 3 * PATTERN_WIDTH
D_FF = 4 * D_MODEL
DT_PER_GROUP = 2 * SSD_HEADS_PER_GROUP
DT_ROWS = SSD_GROUPS * DT_PER_GROUP
ALPHA = 2.0 ** 0.25
NEG_BIG = -1e30
LOG2E = 1.4426950408889634

V7X_VMEM_LIMIT = 56 * 1024 * 1024
LANES = 128
SUBLANES = 8
BF16_SUBLANES = 16
CONV_ROW_STRIDE = 4

_OFF_Z = 0
_OFF_XBC = _OFF_Z + D_INNER
_OFF_QKV = _OFF_XBC + CONV_DIM
_OFF_GATE = _OFF_QKV + 3 * QKV_WIDTH
IN_COLS_PACKED = _OFF_GATE + 2 * D_MODEL


def _group_xbc(a):
    nb = SSD_GROUPS * D_STATE
    xs, bs, cs = a[..., :D_INNER], a[..., D_INNER:D_INNER + nb], a[..., D_INNER + nb:]
    parts = []
    for g in range(SSD_GROUPS):
        parts += [xs[..., g * GROUP_X:(g + 1) * GROUP_X], bs[..., g * D_STATE:(g + 1) * D_STATE],
                  cs[..., g * D_STATE:(g + 1) * D_STATE]]
    return jnp.concatenate(parts, axis=-1)


def _pack_w_in(w_in):
    splits = np.cumsum([D_INNER, CONV_DIM, 32, 32, ATTN_WIDTH, ATTN_WIDTH, ATTN_WIDTH])
    wz, wxbc, wdtf, wdtb, wq, wk, wv, wg = jnp.split(w_in, list(splits), axis=1)
    wxbc = _group_xbc(wxbc)
    wq = wq * (1.0 / math.sqrt(ATTN_HEAD_DIM))
    qkv = []
    for p in range(3):
        s = slice(p * PATTERN_WIDTH, (p + 1) * PATTERN_WIDTH)
        qkv += [wq[:, s], wk[:, s], wv[:, s]]
    dts = []
    for g in range(SSD_GROUPS):
        s = slice(g * SSD_HEADS_PER_GROUP, (g + 1) * SSD_HEADS_PER_GROUP)
        dts += [wdtf[:, s], wdtb[:, s]]
    packed = jnp.concatenate([wz, wxbc] + qkv + [wg], axis=1).astype(jnp.bfloat16)
    wdt_t = jnp.concatenate(dts, axis=1).T.astype(jnp.bfloat16)
    return packed, wdt_t


def _in_proj_kernel(x_ref, w_ref, wdt_t_ref, z_ref, xbc_ref, qkv1_ref, qkv2_ref, qkv3_ref, gate_ref,
                    dtt_ref, stage_ref, *, tm, nc):
    xb = x_ref[0].astype(jnp.bfloat16)

    def proj(off, width):
        return jnp.dot(xb, w_ref[:, off:off + width], preferred_element_type=jnp.float32)

    for c in range(0, D_INNER, nc):
        z_ref[0, :, c:c + nc] = proj(_OFF_Z + c, nc).astype(z_ref.dtype)
    for c in range(0, CONV_DIM, nc):
        xbc_ref[0, :, c:c + nc] = proj(_OFF_XBC + c, nc).astype(xbc_ref.dtype)
    for c in range(0, 2 * D_MODEL, nc):
        gate_ref[0, :, c:c + nc] = proj(_OFF_GATE + c, nc).astype(gate_ref.dtype)
    qkv1_ref[0] = proj(_OFF_QKV, QKV_WIDTH).astype(qkv1_ref.dtype)
    for out_ref, pat, dil in ((qkv2_ref, 1, DILATIONS[1]), (qkv3_ref, 2, DILATIONS[2])):
        res = proj(_OFF_QKV + pat * QKV_WIDTH, QKV_WIDTH)
        for c in range(QKV_WIDTH // LANES):
            stage_ref[c] = res[:, c * LANES:(c + 1) * LANES]
        for r in range(dil):
            for c in range(QKV_WIDTH // LANES):
                out_ref[0, r, :, c * LANES:(c + 1) * LANES] = (
                    stage_ref[c, pl.ds(r, tm // dil, stride=dil), :].astype(out_ref.dtype))
    dtt = lax.dot_general(wdt_t_ref[...], xb, (((1,), (1,)), ((), ())),
                          preferred_element_type=jnp.float32)
    for g in range(SSD_GROUPS):
        dtt_ref[0, g] = dtt[g * DT_PER_GROUP:(g + 1) * DT_PER_GROUP, :]


def _in_proj(x, w_packed, wdt_t, *, tm=512, nc=512):
    bsz, s, d = x.shape
    assert s % tm == 0 and tm % (BF16_SUBLANES * DILATIONS[2]) == 0
    act = jnp.bfloat16
    out_shape = (
        jax.ShapeDtypeStruct((bsz, s, D_INNER), act),
        jax.ShapeDtypeStruct((bsz, s, CONV_DIM), act),
        jax.ShapeDtypeStruct((bsz, s, QKV_WIDTH), act),
        jax.ShapeDtypeStruct((bsz, DILATIONS[1], s // DILATIONS[1], QKV_WIDTH), act),
        jax.ShapeDtypeStruct((bsz, DILATIONS[2], s // DILATIONS[2], QKV_WIDTH), act),
        jax.ShapeDtypeStruct((bsz, s, 2 * D_MODEL), act),
        jax.ShapeDtypeStruct((bsz, SSD_GROUPS, DT_PER_GROUP, s), jnp.float32),
    )
    tok = lambda width: pl.BlockSpec((1, tm, width), lambda b, i: (b, i, 0))
    resident = lambda shape: pl.BlockSpec(shape, lambda b, i: (0,) * len(shape),
                                          pipeline_mode=pl.Buffered(1))
    return pl.pallas_call(
        functools.partial(_in_proj_kernel, tm=tm, nc=nc),
        grid=(bsz, s // tm),
        in_specs=[tok(d), resident(w_packed.shape), resident(wdt_t.shape)],
        out_specs=(
            tok(D_INNER), tok(CONV_DIM), tok(QKV_WIDTH),
            pl.BlockSpec((1, DILATIONS[1], tm // DILATIONS[1], QKV_WIDTH), lambda b, i: (b, 0, i, 0)),
            pl.BlockSpec((1, DILATIONS[2], tm // DILATIONS[2], QKV_WIDTH), lambda b, i: (b, 0, i, 0)),
            tok(2 * D_MODEL),
            pl.BlockSpec((1, SSD_GROUPS, DT_PER_GROUP, tm), lambda b, i: (b, 0, 0, i)),
        ),
        out_shape=out_shape,
        scratch_shapes=[pltpu.VMEM((QKV_WIDTH // LANES, tm, LANES), jnp.float32)],
        compiler_params=pltpu.CompilerParams(
            dimension_semantics=("arbitrary", "arbitrary"), vmem_limit_bytes=V7X_VMEM_LIMIT),
        name="in_proj",
    )(x, w_packed, wdt_t)


def _sigmoid(v):
    return 0.5 * (1.0 + jnp.tanh(0.5 * v))


def _split3_bf16(a):
    hi = a.astype(jnp.bfloat16)
    r1 = a - hi.astype(jnp.float32)
    mid = r1.astype(jnp.bfloat16)
    lo = (r1 - mid.astype(jnp.float32)).astype(jnp.bfloat16)
    return hi, mid, lo


_ROW_CUM, _ROW_E, _ROW_W = 32, 56, 80
_STACK_PARTS = 3
_BCAST_COLS = SSD_HEADS_PER_GROUP * SSD_CHUNK + 2 * GROUP_X
LOG2_DT_FLOOR = -1e4


def _ssd_select_matrix():
    r = np.zeros((SSD_CHUNK, _BCAST_COLS), np.float32)
    hd, q = SSD_HEADS_PER_GROUP, SSD_CHUNK
    for h in range(hd):
        for part in range(_STACK_PARTS):
            r[_ROW_CUM + part * hd + h, h * q:(h + 1) * q] = 1.0
            e0 = hd * q + h * SSD_HEAD_DIM
            r[_ROW_E + part * hd + h, e0:e0 + SSD_HEAD_DIM] = 1.0
            r[_ROW_W + part * hd + h, e0 + GROUP_X:e0 + GROUP_X + SSD_HEAD_DIM] = 1.0
    return jnp.asarray(r, dtype=jnp.bfloat16)


def _ssd_blockdiag_mask():
    hd, q = SSD_HEADS_PER_GROUP, SSD_CHUNK
    m = np.zeros((_ROW_CUM, hd * q), np.float32)
    for part in range(_STACK_PARTS):
        for h in range(hd):
            m[part * hd + h, h * q:(h + 1) * q] = 1.0
    return jnp.asarray(m)


def _ssd_decay_terms(dt_raw_t, dt_bias_c, a_log_c, bdmask_ref, *, backward):
    q, hd = SSD_CHUNK, SSD_HEADS_PER_GROUP
    f32, bf16 = jnp.float32, jnp.bfloat16
    row = lax.broadcasted_iota(jnp.int32, (q, q), 0)
    col = lax.broadcasted_iota(jnp.int32, (q, q), 1)
    tri = ((row >= col) if backward else (row <= col)).astype(f32).astype(bf16)
    end = 0 if backward else q - 1
    v = dt_raw_t + dt_bias_c
    dt = jnp.maximum(v, 0.0) + jnp.log1p(jnp.exp(-jnp.abs(v)))
    a2 = dt * (-jnp.exp(a_log_c) * LOG2E)
    cum3 = jnp.dot(jnp.concatenate(_split3_bf16(a2), axis=0), tri, preferred_element_type=f32)
    cum2 = cum3[0:hd] + cum3[hd:2 * hd] + cum3[2 * hd:3 * hd]
    tot2 = cum2[:, end:end + 1]
    e_r = jnp.exp2(cum2)
    w_r = dt * jnp.exp2(tot2 - cum2)
    nrowp = jnp.maximum(jnp.log2(dt), LOG2_DT_FLOOR) - cum2
    stacked = jnp.concatenate((jnp.ones((_ROW_CUM, q), bf16),) + _split3_bf16(cum2) + _split3_bf16(e_r)
                              + _split3_bf16(w_r) + (jnp.zeros((q - _ROW_W - _STACK_PARTS * hd, q), bf16),),
                              axis=0)
    cols = stacked.astype(f32).T.astype(bf16)
    nparts = jnp.concatenate(_split3_bf16(nrowp) + (jnp.zeros((hd, q), bf16),), axis=0).astype(f32)
    data_rows = (jnp.tile(nparts, (1, hd)) * bdmask_ref[...]).astype(bf16)
    return cols, data_rows


def _ssd_intra(xq, bq, cq, cols, data_rows, sel_ref, w_exp, *, backward):
    q, hd = SSD_CHUNK, SSD_HEADS_PER_GROUP
    f32, bf16 = jnp.float32, jnp.bfloat16
    row = lax.broadcasted_iota(jnp.int32, (q, q), 0)
    col = lax.broadcasted_iota(jnp.int32, (q, q), 1)
    mask = (row <= col) if backward else (row >= col)
    r_diff = jnp.concatenate([data_rows, sel_ref[_ROW_CUM:, 0:hd * q]], axis=0)
    big_diff = jnp.dot(cols, r_diff, preferred_element_type=f32)
    cb = lax.dot_general(cq, bq, (((1,), (1,)), ((), ())), preferred_element_type=f32)
    lane = lax.broadcasted_iota(jnp.int32, (q, LANES), 1)
    first_half = lane < SSD_HEAD_DIM
    ys = []
    for j in range(hd // 2):
        xpair = xq[:, j * LANES:(j + 1) * LANES]
        ms = []
        for h in (2 * j, 2 * j + 1):
            diff = big_diff[:, h * q:(h + 1) * q]
            ms.append((cb * jnp.exp2(jnp.where(mask, diff, NEG_BIG))).astype(bf16))
        lhs = jnp.concatenate(ms, axis=1)
        zero = jnp.zeros_like(xpair)
        rhs = jnp.concatenate([jnp.where(first_half, xpair, zero),
                               jnp.where(first_half, zero, xpair)], axis=0)
        ys.append(jnp.dot(lhs, rhs, preferred_element_type=f32))
    xw = (xq.astype(f32) * w_exp).astype(bf16)
    s_new = lax.dot_general(bq, xw, (((0,), (0,)), ((), ())), preferred_element_type=f32)
    return jnp.concatenate(ys, axis=1), s_new


def _ssd_kernel(xbc_ref, prev_ref, next_ref, dtt_ref, z_ref, convw_ref, convb_ref, dtb_ref, alog_ref,
                dskip_ref, normw_ref, sel_ref, bdmask_ref, y_ref, ext_ref, stage_ref, xc_ref, yf_ref, h_ref,
                ew_ref, yd_ref, sn_ref, *, lb, nb):
    ph = pl.program_id(2)
    i = pl.program_id(3)
    q = SSD_CHUNK
    hd = SSD_HEADS_PER_GROUP
    nck = lb // q
    f32, bf16 = jnp.float32, jnp.bfloat16
    halo = BF16_SUBLANES
    pad = D_CONV // 2
    nslab = GROUP_XBC // LANES

    def scan_block(base, *, backward, finish):
        hs = slice(hd, DT_PER_GROUP) if backward else slice(0, hd)
        order = list(reversed(range(nck))) if backward else list(range(nck))
        end = 0 if backward else q - 1
        rows = lambda c: pl.ds(base + c * q, q)
        blk = lambda c: slice(c * q, (c + 1) * q)
        terms = [_ssd_decay_terms(dtt_ref[0, 0, hs, blk(c)], dtb_ref[0, hs, :], alog_ref[0, hs, :],
                                  bdmask_ref, backward=backward) for c in range(nck)]
        for c in range(nck):
            ew_ref[blk(c), :] = jnp.dot(terms[c][0], sel_ref[:, hd * q:], preferred_element_type=f32)
        for c in range(nck):
            r = rows(c)
            yd, sn = _ssd_intra(xc_ref[r, 0:GROUP_X], xc_ref[r, GROUP_X:GROUP_X + D_STATE],
                                xc_ref[r, GROUP_X + D_STATE:GROUP_XBC], terms[c][0], terms[c][1], sel_ref,
                                ew_ref[blk(c), GROUP_X:], backward=backward)
            yd_ref[blk(c), :] = yd
            sn_ref[c] = sn
        h = h_ref[...]
        for c in order:
            ch = jnp.dot(xc_ref[rows(c), GROUP_X + D_STATE:GROUP_XBC], h.astype(bf16), preferred_element_type=f32)
            finish(c, yd_ref[blk(c), :] + ew_ref[blk(c), 0:GROUP_X] * ch)
            decay = ew_ref[c * q + end:c * q + end + 1, 0:GROUP_X]
            h = decay * h + sn_ref[c]
        h_ref[...] = h

    @pl.when(i == 0)
    def _():
        h_ref[...] = jnp.zeros_like(h_ref)

    @pl.when(ph == 0)
    def _forward():
        base = pl.multiple_of(i * lb, lb)
        for s in range(nslab):
            cs = slice(s * LANES, (s + 1) * LANES)
            ext_ref[s, 0:halo, :] = jnp.where(i > 0, prev_ref[0, :, cs].astype(f32), 0.0)
            ext_ref[s, halo:halo + lb, :] = xbc_ref[0, :, cs].astype(f32)
            ext_ref[s, halo + lb:2 * halo + lb, :] = jnp.where(i < nb - 1, next_ref[0, :, cs].astype(f32), 0.0)
        for s in range(nslab):
            cs = slice(s * LANES, (s + 1) * LANES)
            taps = [jnp.broadcast_to(convw_ref[0, k:k + 1, cs], (SUBLANES, LANES)) for k in range(D_CONV)]
            bias = jnp.broadcast_to(convb_ref[0, :, cs], (SUBLANES, LANES))
            for r0 in range(0, lb, SUBLANES * CONV_ROW_STRIDE):
                for v in range(CONV_ROW_STRIDE):
                    acc = bias
                    for k in range(D_CONV):
                        lo = halo - pad + k + r0 + v
                        acc = acc + taps[k] * ext_ref[s, pl.ds(lo, SUBLANES, stride=CONV_ROW_STRIDE), :]
                    stage_ref[s, pl.ds(r0 + v, SUBLANES, stride=CONV_ROW_STRIDE), :] = acc * _sigmoid(acc)
            xc_ref[pl.ds(base, lb), cs] = stage_ref[s].astype(xc_ref.dtype)

        def finish(c, y):
            yf_ref[pl.ds(base + c * q, q), :] = y

        scan_block(base, backward=False, finish=finish)

    @pl.when(ph == 1)
    def _backward():
        base = pl.multiple_of((nb - 1 - i) * lb, lb)

        def finish(c, y):
            r = pl.ds(base + c * q, q)
            y = y + yf_ref[r, :] + dskip_ref[0] * xc_ref[r, 0:GROUP_X].astype(f32)
            zq = z_ref[0, c * q:(c + 1) * q, :].astype(f32)
            y = y * (zq * _sigmoid(zq))
            ms = jnp.mean(y * y, axis=-1, keepdims=True)
            y_ref[0, c * q:(c + 1) * q, :] = (y * lax.rsqrt(ms + NORM_EPS) * normw_ref[0]).astype(y_ref.dtype)

        scan_block(base, backward=True, finish=finish)


def _ssd(xbc, dtt, z, conv_w, conv_b, dt_bias_f, dt_bias_b, a_log_f, a_log_b, d_skip, ssd_norm_w, *, lb=512):
    bsz, s, _ = xbc.shape
    nb = s // lb
    assert s % lb == 0 and lb % SSD_CHUNK == 0
    halo = BF16_SUBLANES
    hb = lb // halo
    f32 = jnp.float32
    convw = _group_xbc(conv_w.astype(f32)).reshape(D_CONV, SSD_GROUPS, GROUP_XBC).transpose(1, 0, 2)
    convb = _group_xbc(conv_b.astype(f32)).reshape(SSD_GROUPS, 1, GROUP_XBC)
    per_group = lambda f, b: jnp.concatenate(
        [f.astype(f32).reshape(SSD_GROUPS, SSD_HEADS_PER_GROUP), b.astype(f32).reshape(SSD_GROUPS, SSD_HEADS_PER_GROUP)],
        axis=1).reshape(SSD_GROUPS, DT_PER_GROUP, 1)
    dtb = per_group(dt_bias_f, dt_bias_b)
    alog = per_group(a_log_f, a_log_b)
    dskip = jnp.repeat(d_skip.astype(f32), SSD_HEAD_DIM).reshape(SSD_GROUPS, 1, GROUP_X)
    normw = ssd_norm_w.astype(f32).reshape(SSD_GROUPS, 1, GROUP_X)
    sel, bdmask = _ssd_select_matrix(), _ssd_blockdiag_mask()

    fwd_blk = lambda ph, i: jnp.where(ph == 0, i, nb - 1)
    any_blk = lambda ph, i: jnp.where(ph == 0, i, nb - 1 - i)
    bwd_blk = lambda ph, i: jnp.where(ph == 0, nb - 1, nb - 1 - i)
    par = lambda shape: pl.BlockSpec(shape, lambda b, g, ph, i: (g, 0, 0))
    return pl.pallas_call(
        functools.partial(_ssd_kernel, lb=lb, nb=nb),
        grid=(bsz, SSD_GROUPS, 2, nb),
        in_specs=[
            pl.BlockSpec((1, lb, GROUP_XBC), lambda b, g, ph, i: (b, fwd_blk(ph, i), g)),
            pl.BlockSpec((1, halo, GROUP_XBC), lambda b, g, ph, i: (b, jnp.maximum(fwd_blk(ph, i) * hb - 1, 0), g)),
            pl.BlockSpec((1, halo, GROUP_XBC),
                         lambda b, g, ph, i: (b, jnp.minimum((fwd_blk(ph, i) + 1) * hb, nb * hb - 1), g)),
            pl.BlockSpec((1, 1, DT_PER_GROUP, lb), lambda b, g, ph, i: (b, g, 0, any_blk(ph, i))),
            pl.BlockSpec((1, lb, GROUP_X), lambda b, g, ph, i: (b, bwd_blk(ph, i), g)),
            par((1, D_CONV, GROUP_XBC)), par((1, 1, GROUP_XBC)),
            par((1, DT_PER_GROUP, 1)), par((1, DT_PER_GROUP, 1)),
            par((1, 1, GROUP_X)), par((1, 1, GROUP_X)),
            pl.BlockSpec(sel.shape, lambda b, g, ph, i: (0, 0)),
            pl.BlockSpec(bdmask.shape, lambda b, g, ph, i: (0, 0)),
        ],
        out_specs=pl.BlockSpec((1, lb, GROUP_X), lambda b, g, ph, i: (b, bwd_blk(ph, i), g)),
        out_shape=jax.ShapeDtypeStruct((bsz, s, D_INNER), jnp.bfloat16),
        scratch_shapes=[
            pltpu.VMEM((GROUP_XBC // LANES, lb + 2 * halo, LANES), f32),
            pltpu.VMEM((GROUP_XBC // LANES, lb, LANES), f32),
            pltpu.VMEM((s, GROUP_XBC), jnp.bfloat16),
            pltpu.VMEM((s, GROUP_X), f32),
            pltpu.VMEM((D_STATE, GROUP_X), f32),
            pltpu.VMEM((lb, 2 * GROUP_X), f32),
            pltpu.VMEM((lb, GROUP_X), f32),
            pltpu.VMEM((lb // SSD_CHUNK, D_STATE, GROUP_X), f32),
        ],
        compiler_params=pltpu.CompilerParams(
            dimension_semantics=("arbitrary",) * 4, vmem_limit_bytes=V7X_VMEM_LIMIT),
        name="ssd",
    )(xbc, xbc, xbc, dtt, z, convw, convb, dtb, alog, dskip, normw, sel, bdmask)


ATTN_QB = 128
ATTN_WIN = ATTN_QB + 2 * ATTN_HALF


def _attn_bias_table():
    heads = len(DILATIONS) * HEADS_PER_PATTERN
    slopes = 2.0 ** (-8.0 * np.arange(1, heads + 1) / heads)
    rel = (np.arange(ATTN_WIN)[None, :] - ATTN_HALF) - np.arange(ATTN_QB)[:, None]
    tabs = []
    for h in range(heads):
        dil = DILATIONS[h // HEADS_PER_PATTERN]
        tabs.append(np.where(np.abs(rel) <= ATTN_HALF, -slopes[h] * np.abs(rel) * dil, NEG_BIG))
    return jnp.asarray(np.stack(tabs), dtype=jnp.float32)


def _attn_kernel(bias_ref, q1_ref, p1_ref, n1_ref, q2_ref, p2_ref, n2_ref, q3_ref, p3_ref, n3_ref, y_ref,
                 kv1_ref, kv2_ref, kv3_ref, o_ref, lse_ref, *, ta, nt, seq):
    i = pl.program_id(1)
    f32, bf16 = jnp.float32, jnp.bfloat16
    kvw = 2 * PATTERN_WIDTH
    lane = lax.broadcasted_iota(jnp.int32, (ATTN_QB, LANES), 1)
    first_half = lane < ATTN_HEAD_DIM
    kcol = lax.broadcasted_iota(jnp.int32, (1, ATTN_WIN), 1)

    for pat, (dil, q_ref, prev_ref, next_ref, kv_ref) in enumerate((
            (DILATIONS[0], q1_ref, p1_ref, n1_ref, kv1_ref),
            (DILATIONS[1], q2_ref, p2_ref, n2_ref, kv2_ref),
            (DILATIONS[2], q3_ref, p3_ref, n3_ref, kv3_ref))):
        n = ta // dil
        seq_l = seq // dil
        kv_ref[:, 0:ATTN_HALF, :] = prev_ref[0, :, :, PATTERN_WIDTH:]
        kv_ref[:, ATTN_HALF:ATTN_HALF + n, :] = q_ref[0, :, :, PATTERN_WIDTH:]
        kv_ref[:, ATTN_HALF + n:2 * ATTN_HALF + n, :] = next_ref[0, :, :, PATTERN_WIDTH:]
        nqb = n // ATTN_QB

        def block(idx, carry, dil=dil, q_ref=q_ref, kv_ref=kv_ref, nqb=nqb, n=n, seq_l=seq_l, pat=pat):
            r = idx // nqb
            jb = idx % nqb
            row0 = pl.multiple_of(jb * ATTN_QB, ATTN_QB)
            kpos = i * n + row0 - ATTN_HALF + kcol
            pen = jnp.where((kpos >= 0) & (kpos < seq_l), 0.0, NEG_BIG).astype(f32)
            for pair in range(HEADS_PER_PATTERN // 2):
                cs = slice(pair * LANES, (pair + 1) * LANES)
                qp = q_ref[0, r, pl.ds(row0, ATTN_QB), cs]
                kp = kv_ref[r, pl.ds(row0, ATTN_WIN), cs]
                vp = kv_ref[r, pl.ds(row0, ATTN_WIN), pair * LANES + PATTERN_WIDTH:(pair + 1) * LANES + PATTERN_WIDTH]
                zero = jnp.zeros_like(qp)
                outs, lses = [], []
                for sub in range(2):
                    head = pat * HEADS_PER_PATTERN + 2 * pair + sub
                    qh = jnp.where(first_half, qp, zero) if sub == 0 else jnp.where(first_half, zero, qp)
                    sc = lax.dot_general(qh, kp, (((1,), (1,)), ((), ())), preferred_element_type=f32)
                    sc = sc + bias_ref[head] + pen
                    m = jnp.max(sc, axis=-1, keepdims=True)
                    pe = jnp.exp(sc - m)
                    den = jnp.sum(pe, axis=-1, keepdims=True)
                    outs.append(jnp.dot(pe.astype(bf16), vp, preferred_element_type=f32) / den)
                    lses.append(m + jnp.log(den))
                rows = pl.ds(r + row0 * dil, ATTN_QB, stride=dil)
                o_ref[pat, pair, rows, :] = jnp.where(first_half, outs[0], outs[1])
                lse_ref[pat, pair, rows, :] = jnp.where(first_half, lses[0], lses[1])
            return carry

        lax.fori_loop(0, dil * nqb, block, 0)

    for pair in range(HEADS_PER_PATTERN // 2):
        l0, l1, l2 = lse_ref[0, pair], lse_ref[1, pair], lse_ref[2, pair]
        m = jnp.maximum(jnp.maximum(l0, l1), l2)
        w0, w1, w2 = jnp.exp(l0 - m), jnp.exp(l1 - m), jnp.exp(l2 - m)
        y = (w0 * o_ref[0, pair] + w1 * o_ref[1, pair] + w2 * o_ref[2, pair]) / (w0 + w1 + w2)
        y_ref[0, :, pair * LANES:(pair + 1) * LANES] = y.astype(y_ref.dtype)


def _attention(qkv1, qkv2, qkv3, *, ta=2048):
    bsz, s, _ = qkv1.shape
    nt = s // ta
    assert s % ta == 0 and ta % (ATTN_QB * DILATIONS[2]) == 0
    qkv1 = qkv1.reshape(bsz, 1, s, QKV_WIDTH)
    hb = ATTN_HALF
    specs = []
    for dil in DILATIONS:
        n = ta // dil
        last = s // dil // hb - 1
        specs += [
            pl.BlockSpec((1, dil, n, QKV_WIDTH), lambda b, i: (b, 0, i, 0)),
            pl.BlockSpec((1, dil, hb, QKV_WIDTH), lambda b, i, n=n: (b, 0, jnp.maximum(i * (n // hb) - 1, 0), 0)),
            pl.BlockSpec((1, dil, hb, QKV_WIDTH),
                         lambda b, i, n=n, last=last: (b, 0, jnp.minimum((i + 1) * (n // hb), last), 0)),
        ]
    bias = _attn_bias_table()
    kvw = 2 * PATTERN_WIDTH
    return pl.pallas_call(
        functools.partial(_attn_kernel, ta=ta, nt=nt, seq=s),
        grid=(bsz, nt),
        in_specs=[pl.BlockSpec(bias.shape, lambda b, i: (0, 0, 0))] + specs,
        out_specs=pl.BlockSpec((1, ta, PATTERN_WIDTH), lambda b, i: (b, i, 0)),
        out_shape=jax.ShapeDtypeStruct((bsz, s, PATTERN_WIDTH), jnp.bfloat16),
        scratch_shapes=[pltpu.VMEM((dil, ta // dil + 2 * hb, kvw), jnp.bfloat16) for dil in DILATIONS] + [
            pltpu.VMEM((len(DILATIONS), PATTERN_WIDTH // LANES, ta, LANES), jnp.float32),
            pltpu.VMEM((len(DILATIONS), PATTERN_WIDTH // LANES, ta, LANES), jnp.float32),
        ],
        compiler_params=pltpu.CompilerParams(
            dimension_semantics=("arbitrary", "arbitrary"), vmem_limit_bytes=V7X_VMEM_LIMIT),
        name="dilated_attn",
    )(bias, qkv1, qkv1, qkv1, qkv2, qkv2, qkv2, qkv3, qkv3, qkv3)


def _layer_norm(v, g, b):
    mu = jnp.mean(v, axis=-1, keepdims=True)
    c = v - mu
    var = jnp.mean(c * c, axis=-1, keepdims=True)
    return c * lax.rsqrt(var + NORM_EPS) * g + b


def _merge_mlp_kernel(x_ref, yssd_ref, yatt_ref, gate_ref, bgate_ref, wps_ref, wpa_ref, wout_ref,
                      ln1g_ref, ln1b_ref, wup_ref, wdown_ref, ln2g_ref, ln2b_ref, o_ref, *, ffc):
    f32, bf16 = jnp.float32, jnp.bfloat16
    ys = jnp.dot(yssd_ref[0], wps_ref[...], preferred_element_type=f32)
    ya = jnp.dot(yatt_ref[0], wpa_ref[...], preferred_element_type=f32)
    g_ssd = _sigmoid(gate_ref[0, :, 0:D_MODEL].astype(f32) + bgate_ref[:, 0:D_MODEL])
    g_att = _sigmoid(gate_ref[0, :, D_MODEL:2 * D_MODEL].astype(f32) + bgate_ref[:, D_MODEL:2 * D_MODEL])
    mix = jnp.dot((g_ssd * ys + g_att * ya).astype(bf16), wout_ref[...], preferred_element_type=f32)
    h = _layer_norm(ALPHA * x_ref[0] + mix, ln1g_ref[...], ln1b_ref[...])
    hb = h.astype(bf16)
    f = jnp.zeros_like(h)
    for c in range(0, D_FF, ffc):
        up = jnp.maximum(jnp.dot(hb, wup_ref[:, c:c + ffc], preferred_element_type=f32), 0.0)
        f = f + jnp.dot((up * up).astype(bf16), wdown_ref[c:c + ffc, :], preferred_element_type=f32)
    o_ref[0] = _layer_norm(ALPHA * h + f, ln2g_ref[...], ln2b_ref[...]).astype(o_ref.dtype)


def _merge_mlp(x, y_ssd, y_att, gate, b_gate, w_proj_ssd, w_proj_attn, w_out, ln1_g, ln1_b, w_up, w_down,
               ln2_g, ln2_b, *, tm=512, ffc=1024):
    bsz, s, d = x.shape
    assert s % tm == 0 and D_FF % ffc == 0
    bf16, f32 = jnp.bfloat16, jnp.float32
    row = lambda v: v.astype(f32).reshape(1, -1)
    tok = lambda width: pl.BlockSpec((1, tm, width), lambda b, i: (b, i, 0))
    resident = lambda shape: pl.BlockSpec(shape, lambda b, i: (0,) * len(shape), pipeline_mode=pl.Buffered(1))
    operands = [
        (x, tok(d)), (y_ssd, tok(D_INNER)), (y_att, tok(PATTERN_WIDTH)), (gate, tok(2 * D_MODEL)),
        (row(b_gate), None), (w_proj_ssd.astype(bf16), None), (w_proj_attn.astype(bf16), None),
        (w_out.astype(bf16), None), (row(ln1_g), None), (row(ln1_b), None),
        (w_up.astype(bf16), None), (w_down.astype(bf16), None), (row(ln2_g), None), (row(ln2_b), None),
    ]
    args = [a for a, _ in operands]
    in_specs = [spec if spec is not None else resident(a.shape) for a, spec in operands]
    return pl.pallas_call(
        functools.partial(_merge_mlp_kernel, ffc=ffc),
        grid=(bsz, s // tm),
        in_specs=in_specs,
        out_specs=tok(d),
        out_shape=jax.ShapeDtypeStruct((bsz, s, d), x.dtype),
        compiler_params=pltpu.CompilerParams(
            dimension_semantics=("arbitrary", "arbitrary"), vmem_limit_bytes=V7X_VMEM_LIMIT),
        name="merge_mlp",
    )(*args)


def kernel(x, w_in, b_gate, conv_w, conv_b, dt_bias_f, dt_bias_b, a_log_f, a_log_b, d_skip, ssd_norm_w,
           w_proj_ssd, w_proj_attn, w_out, ln1_g, ln1_b, w_up, w_down, ln2_g, ln2_b):
    w_packed, wdt_t = _pack_w_in(w_in)
    z, xbc, qkv1, qkv2, qkv3, gate, dtt = _in_proj(x, w_packed, wdt_t)
    y_ssd = _ssd(xbc, dtt, z, conv_w, conv_b, dt_bias_f, dt_bias_b, a_log_f, a_log_b, d_skip, ssd_norm_w)
    y_att = _attention(qkv1, qkv2, qkv3)
    return _merge_mlp(x, y_ssd, y_att, gate, b_gate, w_proj_ssd, w_proj_attn, w_out, ln1_g, ln1_b,
                      w_up, w_down, ln2_g, ln2_b)
```

```python
import functools
import math

import jax
import jax.numpy as jnp
import numpy as np
from jax import lax
from jax.experimental import pallas as pl
from jax.experimental.pallas import tpu as pltpu

D_MODEL = 1024
D_INNER = 2048
SSD_HEAD_DIM = 64
SSD_GROUPS = 4
SSD_HEADS_PER_GROUP = 8
D_STATE = 128
D_CONV = 5
SSD_CHUNK = 128
GROUP_X = SSD_HEADS_PER_GROUP * SSD_HEAD_DIM
GROUP_XBC = GROUP_X + 2 * D_STATE
CONV_DIM = D_INNER + 2 * SSD_GROUPS * D_STATE
NORM_EPS = 1e-5
ATTN_HEAD_DIM = 64
DILATIONS = (1, 4, 16)
ATTN_HALF = 64
HEADS_PER_PATTERN = 4
PATTERN_WIDTH = HEADS_PER_PATTERN * ATTN_HEAD_DIM
QKV_WIDTH = 3 * PATTERN_WIDTH
ATTN_WIDTH = 3 * PATTERN_WIDTH
D_FF = 4 * D_MODEL
DT_PER_GROUP = 2 * SSD_HEADS_PER_GROUP
DT_ROWS = SSD_GROUPS * DT_PER_GROUP
ALPHA = 2.0 ** 0.25
NEG_BIG = -1e30
LOG2E = 1.4426950408889634

V7X_VMEM_LIMIT = 56 * 1024 * 1024
LANES = 128
SUBLANES = 8
BF16_SUBLANES = 16
CONV_ROW_STRIDE = 4

_OFF_Z = 0
_OFF_XBC = _OFF_Z + D_INNER
_OFF_QKV = _OFF_XBC + CONV_DIM
_OFF_GATE = _OFF_QKV + 3 * QKV_WIDTH
IN_COLS_PACKED = _OFF_GATE + 2 * D_MODEL


def _group_xbc(a):
    nb = SSD_GROUPS * D_STATE
    xs, bs, cs = a[..., :D_INNER], a[..., D_INNER:D_INNER + nb], a[..., D_INNER + nb:]
    parts = []
    for g in range(SSD_GROUPS):
        parts += [xs[..., g * GROUP_X:(g + 1) * GROUP_X], bs[..., g * D_STATE:(g + 1) * D_STATE],
                  cs[..., g * D_STATE:(g + 1) * D_STATE]]
    return jnp.concatenate(parts, axis=-1)


def _pack_w_in(w_in):
    splits = np.cumsum([D_INNER, CONV_DIM, 32, 32, ATTN_WIDTH, ATTN_WIDTH, ATTN_WIDTH])
    wz, wxbc, wdtf, wdtb, wq, wk, wv, wg = jnp.split(w_in, list(splits), axis=1)
    wxbc = _group_xbc(wxbc)
    wq = wq * (1.0 / math.sqrt(ATTN_HEAD_DIM))
    qkv = []
    for p in range(3):
        s = slice(p * PATTERN_WIDTH, (p + 1) * PATTERN_WIDTH)
        qkv += [wq[:, s], wk[:, s], wv[:, s]]
    dts = []
    for g in range(SSD_GROUPS):
        s = slice(g * SSD_HEADS_PER_GROUP, (g + 1) * SSD_HEADS_PER_GROUP)
        dts += [wdtf[:, s], wdtb[:, s]]
    packed = jnp.concatenate([wz, wxbc] + qkv + [wg], axis=1).astype(jnp.bfloat16)
    wdt_t = jnp.concatenate(dts, axis=1).T.astype(jnp.bfloat16)
    return packed, wdt_t


def _in_proj_kernel(x_ref, w_ref, wdt_t_ref, z_ref, xbc_ref, qkv1_ref, qkv2_ref, qkv3_ref, gate_ref,
                    dtt_ref, stage_ref, *, tm, nc):
    xb = x_ref[0].astype(jnp.bfloat16)

    def proj(off, width):
        return jnp.dot(xb, w_ref[:, off:off + width], preferred_element_type=jnp.float32)

    for c in range(0, D_INNER, nc):
        z_ref[0, :, c:c + nc] = proj(_OFF_Z + c, nc).astype(z_ref.dtype)
    for c in range(0, CONV_DIM, nc):
        xbc_ref[0, :, c:c + nc] = proj(_OFF_XBC + c, nc).astype(xbc_ref.dtype)
    for c in range(0, 2 * D_MODEL, nc):
        gate_ref[0, :, c:c + nc] = proj(_OFF_GATE + c, nc).astype(gate_ref.dtype)
    qkv1_ref[0] = proj(_OFF_QKV, QKV_WIDTH).astype(qkv1_ref.dtype)
    for out_ref, pat, dil in ((qkv2_ref, 1, DILATIONS[1]), (qkv3_ref, 2, DILATIONS[2])):
        res = proj(_OFF_QKV + pat * QKV_WIDTH, QKV_WIDTH)
        for c in range(QKV_WIDTH // LANES):
            stage_ref[c] = res[:, c * LANES:(c + 1) * LANES]
        for r in range(dil):
            for c in range(QKV_WIDTH // LANES):
                out_ref[0, r, :, c * LANES:(c + 1) * LANES] = (
                    stage_ref[c, pl.ds(r, tm // dil, stride=dil), :].astype(out_ref.dtype))
    dtt = lax.dot_general(wdt_t_ref[...], xb, (((1,), (1,)), ((), ())),
                          preferred_element_type=jnp.float32)
    for g in range(SSD_GROUPS):
        dtt_ref[0, g] = dtt[g * DT_PER_GROUP:(g + 1) * DT_PER_GROUP, :]


def _in_proj(x, w_packed, wdt_t, *, tm=512, nc=512):
    bsz, s, d = x.shape
    assert s % tm == 0 and tm % (BF16_SUBLANES * DILATIONS[2]) == 0
    act = jnp.bfloat16
    out_shape = (
        jax.ShapeDtypeStruct((bsz, s, D_INNER), act),
        jax.ShapeDtypeStruct((bsz, s, CONV_DIM), act),
        jax.ShapeDtypeStruct((bsz, s, QKV_WIDTH), act),
        jax.ShapeDtypeStruct((bsz, DILATIONS[1], s // DILATIONS[1], QKV_WIDTH), act),
        jax.ShapeDtypeStruct((bsz, DILATIONS[2], s // DILATIONS[2], QKV_WIDTH), act),
        jax.ShapeDtypeStruct((bsz, s, 2 * D_MODEL), act),
        jax.ShapeDtypeStruct((bsz, SSD_GROUPS, DT_PER_GROUP, s), jnp.float32),
    )
    tok = lambda width: pl.BlockSpec((1, tm, width), lambda b, i: (b, i, 0))
    resident = lambda shape: pl.BlockSpec(shape, lambda b, i: (0,) * len(shape),
                                          pipeline_mode=pl.Buffered(1))
    return pl.pallas_call(
        functools.partial(_in_proj_kernel, tm=tm, nc=nc),
        grid=(bsz, s // tm),
        in_specs=[tok(d), resident(w_packed.shape), resident(wdt_t.shape)],
        out_specs=(
            tok(D_INNER), tok(CONV_DIM), tok(QKV_WIDTH),
            pl.BlockSpec((1, DILATIONS[1], tm // DILATIONS[1], QKV_WIDTH), lambda b, i: (b, 0, i, 0)),
            pl.BlockSpec((1, DILATIONS[2], tm // DILATIONS[2], QKV_WIDTH), lambda b, i: (b, 0, i, 0)),
            tok(2 * D_MODEL),
            pl.BlockSpec((1, SSD_GROUPS, DT_PER_GROUP, tm), lambda b, i: (b, 0, 0, i)),
        ),
        out_shape=out_shape,
        scratch_shapes=[pltpu.VMEM((QKV_WIDTH // LANES, tm, LANES), jnp.float32)],
        compiler_params=pltpu.CompilerParams(
            dimension_semantics=("arbitrary", "arbitrary"), vmem_limit_bytes=V7X_VMEM_LIMIT),
        name="in_proj",
    )(x, w_packed, wdt_t)


def _sigmoid(v):
    return 0.5 * (1.0 + jnp.tanh(0.5 * v))


def _split3_bf16(a):
    hi = a.astype(jnp.bfloat16)
    r1 = a - hi.astype(jnp.float32)
    mid = r1.astype(jnp.bfloat16)
    lo = (r1 - mid.astype(jnp.float32)).astype(jnp.bfloat16)
    return hi, mid, lo


_ROW_CUM, _ROW_E, _ROW_W = 32, 56, 80
_STACK_PARTS = 3
_BCAST_COLS = SSD_HEADS_PER_GROUP * SSD_CHUNK + 2 * GROUP_X
LOG2_DT_FLOOR = -1e4


def _ssd_select_matrix():
    r = np.zeros((SSD_CHUNK, _BCAST_COLS), np.float32)
    hd, q = SSD_HEADS_PER_GROUP, SSD_CHUNK
    for h in range(hd):
        for part in range(_STACK_PARTS):
            r[_ROW_CUM + part * hd + h, h * q:(h + 1) * q] = 1.0
            e0 = hd * q + h * SSD_HEAD_DIM
            r[_ROW_E + part * hd + h, e0:e0 + SSD_HEAD_DIM] = 1.0
            r[_ROW_W + part * hd + h, e0 + GROUP_X:e0 + GROUP_X + SSD_HEAD_DIM] = 1.0
    return jnp.asarray(r, dtype=jnp.bfloat16)


def _ssd_blockdiag_mask():
    hd, q = SSD_HEADS_PER_GROUP, SSD_CHUNK
    m = np.zeros((_ROW_CUM, hd * q), np.float32)
    for part in range(_STACK_PARTS):
        for h in range(hd):
            m[part * hd + h, h * q:(h + 1) * q] = 1.0
    return jnp.asarray(m)


def _ssd_decay_terms(dt_raw_t, dt_bias_c, a_log_c, bdmask_ref, *, backward):
    q, hd = SSD_CHUNK, SSD_HEADS_PER_GROUP
    f32, bf16 = jnp.float32, jnp.bfloat16
    row = lax.broadcasted_iota(jnp.int32, (q, q), 0)
    col = lax.broadcasted_iota(jnp.int32, (q, q), 1)
    tri = ((row >= col) if backward else (row <= col)).astype(f32).astype(bf16)
    end = 0 if backward else q - 1
    v = dt_raw_t + dt_bias_c
    dt = jnp.maximum(v, 0.0) + jnp.log1p(jnp.exp(-jnp.abs(v)))
    a2 = dt * (-jnp.exp(a_log_c) * LOG2E)
    cum3 = jnp.dot(jnp.concatenate(_split3_bf16(a2), axis=0), tri, preferred_element_type=f32)
    cum2 = cum3[0:hd] + cum3[hd:2 * hd] + cum3[2 * hd:3 * hd]
    tot2 = cum2[:, end:end + 1]
    e_r = jnp.exp2(cum2)
    w_r = dt * jnp.exp2(tot2 - cum2)
    nrowp = jnp.maximum(jnp.log2(dt), LOG2_DT_FLOOR) - cum2
    stacked = jnp.concatenate((jnp.ones((_ROW_CUM, q), bf16),) + _split3_bf16(cum2) + _split3_bf16(e_r)
                              + _split3_bf16(w_r) + (jnp.zeros((q - _ROW_W - _STACK_PARTS * hd, q), bf16),),
                              axis=0)
    cols = stacked.astype(f32).T.astype(bf16)
    nparts = jnp.concatenate(_split3_bf16(nrowp) + (jnp.zeros((hd, q), bf16),), axis=0).astype(f32)
    data_rows = (jnp.tile(nparts, (1, hd)) * bdmask_ref[...]).astype(bf16)
    return cols, data_rows


def _ssd_intra(xq, bq, cq, big_diff, w_exp, *, backward):
    q, hd = SSD_CHUNK, SSD_HEADS_PER_GROUP
    f32, bf16 = jnp.float32, jnp.bfloat16
    row = lax.broadcasted_iota(jnp.int32, (q, q), 0)
    col = lax.broadcasted_iota(jnp.int32, (q, q), 1)
    mask = (row <= col) if backward else (row >= col)
    cb =lax.dot_general(cq, bq, (((1,), (1,)), ((), ())), preferred_element_type=f32)
    lane = lax.broadcasted_iota(jnp.int32, (q, LANES), 1)
    first_half = lane < SSD_HEAD_DIM
    ys = []
    for j in range(hd // 2):
        xpair = xq[:, j * LANES:(j + 1) * LANES]
        ms = []
        for h in (2 * j, 2 * j + 1):
            diff = big_diff[:, h * q:(h + 1) * q]
            ms.append((cb * jnp.exp2(jnp.where(mask, diff, NEG_BIG))).astype(bf16))
        lhs = jnp.concatenate(ms, axis=1)
        zero = jnp.zeros_like(xpair)
        rhs = jnp.concatenate([jnp.where(first_half, xpair, zero),
                               jnp.where(first_half, zero, xpair)], axis=0)
        ys.append(jnp.dot(lhs, rhs, preferred_element_type=f32))
    xw = xq * w_exp.astype(bf16)
    s_new = lax.dot_general(bq, xw, (((0,), (0,)), ((), ())), preferred_element_type=f32)
    return jnp.concatenate(ys, axis=1), s_new


def _ssd_kernel(xbc_ref, prev_ref, next_ref, dtt_ref, z_ref, convw_ref, convb_ref, dtb_ref, alog_ref,
                dskip_ref, normw_ref, sel_ref, bdmask_ref, y_ref, ext_ref, stage_ref, xc_ref, yf_ref, h_ref,
                ew_ref, *, lb, nb):
    ph = pl.program_id(2)
    i = pl.program_id(3)
    q = SSD_CHUNK
    hd = SSD_HEADS_PER_GROUP
    nck = lb // q
    f32, bf16 = jnp.float32, jnp.bfloat16
    halo = BF16_SUBLANES
    pad = D_CONV // 2
    nslab = GROUP_XBC // LANES

    blk = lambda c: slice(c * q, (c + 1) * q)

    def decay_terms(*, backward):
        hs = slice(hd, DT_PER_GROUP) if backward else slice(0, hd)
        return [_ssd_decay_terms(dtt_ref[0, 0, hs, blk(c)], dtb_ref[0, hs, :], alog_ref[0, hs, :],
                                 bdmask_ref, backward=backward) for c in range(nck)]

    def scan_block(base, terms, *, backward, finish):
        order = list(reversed(range(nck))) if backward else list(range(nck))
        end = 0 if backward else q - 1
        rows = lambda c: pl.ds(base + c * q, q)

        def broadcasts(c):
            cols, data_rows = terms[c]
            ew_ref[blk(c), :] = jnp.dot(cols, sel_ref[:, hd * q:], preferred_element_type=f32)
            r_diff = jnp.concatenate([data_rows, sel_ref[_ROW_CUM:, 0:hd * q]], axis=0)
            return jnp.dot(cols, r_diff, preferred_element_type=f32)

        h = h_ref[...]
        big_diff = broadcasts(order[0])
        for n, c in enumerate(order):
            next_diff = broadcasts(order[n + 1]) if n + 1 < nck else None
            r = rows(c)
            cq = xc_ref[r, GROUP_X + D_STATE:GROUP_XBC]
            yd, sn = _ssd_intra(xc_ref[r, 0:GROUP_X], xc_ref[r, GROUP_X:GROUP_X + D_STATE], cq, big_diff,
                                ew_ref[blk(c), GROUP_X:], backward=backward)
            ch = jnp.dot(cq, h.astype(bf16), preferred_element_type=f32)
            finish(c, yd + ew_ref[blk(c), 0:GROUP_X] * ch)
            decay = ew_ref[c * q + end:c * q + end + 1, 0:GROUP_X]
            h = decay * h + sn
            big_diff = next_diff
        h_ref[...] = h

    @pl.when(i == 0)
    def _():
        h_ref[...] = jnp.zeros_like(h_ref)

    @pl.when(ph == 0)
    def _forward():
        base = pl.multiple_of(i * lb, lb)
        terms = decay_terms(backward=False)
        for s in range(nslab):
            cs = slice(s * LANES, (s + 1) * LANES)
            ext_ref[s, 0:halo, :] = jnp.where(i > 0, prev_ref[0, :, cs].astype(f32), 0.0)
            ext_ref[s, halo:halo + lb, :] = xbc_ref[0, :, cs].astype(f32)
            ext_ref[s, halo + lb:2 * halo + lb, :] = jnp.where(i < nb - 1, next_ref[0, :, cs].astype(f32), 0.0)
        for s in range(nslab):
            cs = slice(s * LANES, (s + 1) * LANES)
            taps = [jnp.broadcast_to(convw_ref[0, k:k + 1, cs], (SUBLANES, LANES)) for k in range(D_CONV)]
            bias = jnp.broadcast_to(convb_ref[0, :, cs], (SUBLANES, LANES))
            for r0 in range(0, lb, SUBLANES * CONV_ROW_STRIDE):
                for v in range(CONV_ROW_STRIDE):
                    acc = bias
                    for k in range(D_CONV):
                        lo = halo - pad + k + r0 + v
                        acc = acc + taps[k] * ext_ref[s, pl.ds(lo, SUBLANES, stride=CONV_ROW_STRIDE), :]
                    stage_ref[s, pl.ds(r0 + v, SUBLANES, stride=CONV_ROW_STRIDE), :] = acc * _sigmoid(acc)
            xc_ref[pl.ds(base, lb), cs] = stage_ref[s].astype(xc_ref.dtype)

        def finish(c, y):
            yf_ref[pl.ds(base + c * q, q), :] = y

        scan_block(base, terms, backward=False, finish=finish)

    @pl.when(ph == 1)
    def _backward():
        base = pl.multiple_of((nb - 1 - i) * lb, lb)

        def finish(c, y):
            r = pl.ds(base + c * q, q)
            y = y + yf_ref[r, :] + dskip_ref[0] * xc_ref[r, 0:GROUP_X].astype(f32)
            zq = z_ref[0, c * q:(c + 1) * q, :].astype(f32)
            y = y * (zq * _sigmoid(zq))
            ms = jnp.mean(y * y, axis=-1, keepdims=True)
            y_ref[0, c * q:(c + 1) * q, :] = (y * lax.rsqrt(ms + NORM_EPS) * normw_ref[0]).astype(y_ref.dtype)

        scan_block(base, decay_terms(backward=True), backward=True, finish=finish)


def _ssd(xbc, dtt, z, conv_w, conv_b, dt_bias_f, dt_bias_b, a_log_f, a_log_b, d_skip, ssd_norm_w, *, lb=1024):
    bsz, s, _ = xbc.shape
    nb = s // lb
    assert s % lb == 0 and lb % SSD_CHUNK == 0
    halo = BF16_SUBLANES
    hb = lb // halo
    f32 = jnp.float32
    convw = _group_xbc(conv_w.astype(f32)).reshape(D_CONV, SSD_GROUPS, GROUP_XBC).transpose(1, 0, 2)
    convb = _group_xbc(conv_b.astype(f32)).reshape(SSD_GROUPS, 1, GROUP_XBC)
    per_group = lambda f, b: jnp.concatenate(
        [f.astype(f32).reshape(SSD_GROUPS, SSD_HEADS_PER_GROUP), b.astype(f32).reshape(SSD_GROUPS, SSD_HEADS_PER_GROUP)],
        axis=1).reshape(SSD_GROUPS, DT_PER_GROUP, 1)
    dtb = per_group(dt_bias_f, dt_bias_b)
    alog = per_group(a_log_f, a_log_b)
    dskip = jnp.repeat(d_skip.astype(f32), SSD_HEAD_DIM).reshape(SSD_GROUPS, 1, GROUP_X)
    normw = ssd_norm_w.astype(f32).reshape(SSD_GROUPS, 1, GROUP_X)
    sel, bdmask = _ssd_select_matrix(), _ssd_blockdiag_mask()

    fwd_blk = lambda ph, i: jnp.where(ph == 0, i, nb - 1)
    any_blk = lambda ph, i: jnp.where(ph == 0, i, nb - 1 - i)
    bwd_blk = lambda ph, i: jnp.where(ph == 0, nb - 1, nb - 1 - i)
    par = lambda shape: pl.BlockSpec(shape, lambda b, g, ph, i: (g, 0, 0))
    return pl.pallas_call(
        functools.partial(_ssd_kernel, lb=lb, nb=nb),
        grid=(bsz, SSD_GROUPS, 2, nb),
        in_specs=[
            pl.BlockSpec((1, lb, GROUP_XBC), lambda b, g, ph, i: (b, fwd_blk(ph, i), g)),
            pl.BlockSpec((1, halo, GROUP_XBC), lambda b, g, ph, i: (b, jnp.maximum(fwd_blk(ph, i) * hb - 1, 0), g)),
            pl.BlockSpec((1, halo, GROUP_XBC),
                         lambda b, g, ph, i: (b, jnp.minimum((fwd_blk(ph, i) + 1) * hb, nb * hb - 1), g)),
            pl.BlockSpec((1, 1, DT_PER_GROUP, lb), lambda b, g, ph, i: (b, g, 0, any_blk(ph, i))),
            pl.BlockSpec((1, lb, GROUP_X), lambda b, g, ph, i: (b, bwd_blk(ph, i), g)),
            par((1, D_CONV, GROUP_XBC)), par((1, 1, GROUP_XBC)),
            par((1, DT_PER_GROUP, 1)), par((1, DT_PER_GROUP, 1)),
            par((1, 1, GROUP_X)), par((1, 1, GROUP_X)),
            pl.BlockSpec(sel.shape, lambda b, g, ph, i: (0, 0)),
            pl.BlockSpec(bdmask.shape, lambda b, g, ph, i: (0, 0)),
        ],
        out_specs=pl.BlockSpec((1, lb, GROUP_X), lambda b, g, ph, i: (b, bwd_blk(ph, i), g)),
        out_shape=jax.ShapeDtypeStruct((bsz, s, D_INNER), jnp.bfloat16),
        scratch_shapes=[
            pltpu.VMEM((GROUP_XBC // LANES, lb + 2 * halo, LANES), f32),
            pltpu.VMEM((GROUP_XBC // LANES, lb, LANES), f32),
            pltpu.VMEM((s, GROUP_XBC), jnp.bfloat16),
            pltpu.VMEM((s, GROUP_X), f32),
            pltpu.VMEM((D_STATE, GROUP_X), f32),
            pltpu.VMEM((lb, 2 * GROUP_X), f32),
        ],
        compiler_params=pltpu.CompilerParams(
            dimension_semantics=("arbitrary",) * 4, vmem_limit_bytes=V7X_VMEM_LIMIT),
        name="ssd",
    )(xbc, xbc, xbc, dtt, z, convw, convb, dtb, alog, dskip, normw, sel, bdmask)


ATTN_QB = 128
ATTN_WIN = ATTN_QB + 2 * ATTN_HALF
ATTN_BLOCKS_PER_ITER = 2


def _attn_bias_table():
    heads = len(DILATIONS) * HEADS_PER_PATTERN
    slopes = 2.0 ** (-8.0 * np.arange(1, heads + 1) / heads)
    rel = (np.arange(ATTN_WIN)[None, :] - ATTN_HALF) - np.arange(ATTN_QB)[:, None]
    tabs = []
    for h in range(heads):
        dil = DILATIONS[h // HEADS_PER_PATTERN]
        tabs.append(np.where(np.abs(rel) <= ATTN_HALF, -slopes[h] * np.abs(rel) * dil, NEG_BIG))
    return jnp.asarray(np.stack(tabs), dtype=jnp.float32)


def _attn_kernel(bias_ref, q1_ref, p1_ref, n1_ref, q2_ref, p2_ref, n2_ref, q3_ref, p3_ref, n3_ref, y_ref,
                 kv1_ref, kv2_ref, kv3_ref, o_ref, lse_ref, *, ta, nt, seq):
    i = pl.program_id(1)
    f32, bf16 = jnp.float32, jnp.bfloat16
    kvw = 2 * PATTERN_WIDTH
    lane = lax.broadcasted_iota(jnp.int32, (ATTN_QB, LANES), 1)
    first_half = lane < ATTN_HEAD_DIM
    kcol = lax.broadcasted_iota(jnp.int32, (1, ATTN_WIN), 1)

    for pat, (dil, q_ref, prev_ref, next_ref, kv_ref) in enumerate((
            (DILATIONS[0], q1_ref, p1_ref, n1_ref, kv1_ref),
            (DILATIONS[1], q2_ref, p2_ref, n2_ref, kv2_ref),
            (DILATIONS[2], q3_ref, p3_ref, n3_ref, kv3_ref))):
        n = ta // dil
        seq_l = seq // dil
        kv_ref[:, 0:ATTN_HALF, :] = prev_ref[0, :, :, PATTERN_WIDTH:]
        kv_ref[:, ATTN_HALF:ATTN_HALF + n, :] = q_ref[0, :, :, PATTERN_WIDTH:]
        kv_ref[:, ATTN_HALF + n:2 * ATTN_HALF + n, :] = next_ref[0, :, :, PATTERN_WIDTH:]
        nqb = n // ATTN_QB

        def block(it, carry, dil=dil, q_ref=q_ref, kv_ref=kv_ref, nqb=nqb, n=n, seq_l=seq_l, pat=pat):
            nh = HEADS_PER_PATTERN
            items = range(ATTN_BLOCKS_PER_ITER * nh)
            places, pens, vps, scs = [], [], [], []
            for k in range(ATTN_BLOCKS_PER_ITER):
                idx = it * ATTN_BLOCKS_PER_ITER + k
                r = idx // nqb
                row0 = pl.multiple_of((idx % nqb) * ATTN_QB, ATTN_QB)
                places.append((r, row0))
                kpos = i * n + row0 - ATTN_HALF + kcol
                pens.append(jnp.where((kpos >= 0) & (kpos < seq_l), 0.0, NEG_BIG).astype(f32))
                for pair in range(nh // 2):
                    cs = slice(pair * LANES, (pair + 1) * LANES)
                    qp = q_ref[0, r, pl.ds(row0, ATTN_QB), cs]
                    kp = kv_ref[r, pl.ds(row0, ATTN_WIN), cs]
                    vps.append(kv_ref[r, pl.ds(row0, ATTN_WIN),
                                      pair * LANES + PATTERN_WIDTH:(pair + 1) * LANES + PATTERN_WIDTH])
                    zero = jnp.zeros_like(qp)
                    for qh in (jnp.where(first_half, qp, zero), jnp.where(first_half, zero, qp)):
                        scs.append(lax.dot_general(qh, kp, (((1,), (1,)), ((), ())), preferred_element_type=f32))
            scs = [scs[t] + bias_ref[pat * nh + t % nh] + pens[t // nh] for t in items]
            ms = [jnp.max(scs[t], axis=-1, keepdims=True) for t in items]
            pes = [jnp.exp(scs[t] - ms[t]) for t in items]
            dens = [jnp.sum(pes[t], axis=-1, keepdims=True) for t in items]
            pvs = [jnp.dot(pes[t].astype(bf16), vps[t // 2], preferred_element_type=f32) for t in items]
            outs = [pvs[t] / dens[t] for t in items]
            lses = [ms[t] + jnp.log(dens[t]) for t in items]
            for k, (r, row0) in enumerate(places):
                rows = pl.ds(r + row0 * dil, ATTN_QB, stride=dil)
                for pair in range(nh // 2):
                    t = k * nh + 2 * pair
                    o_ref[pat, pair, rows, :] = jnp.where(first_half, outs[t], outs[t + 1])
                    lse_ref[pat, pair, rows, :] = jnp.where(first_half, lses[t], lses[t + 1])
            return carry

        lax.fori_loop(0, dil * nqb // ATTN_BLOCKS_PER_ITER, block, 0)

    for pair in range(HEADS_PER_PATTERN // 2):
        l0, l1, l2 = lse_ref[0, pair], lse_ref[1, pair], lse_ref[2, pair]
        m = jnp.maximum(jnp.maximum(l0, l1), l2)
        w0, w1, w2 = jnp.exp(l0 - m), jnp.exp(l1 - m), jnp.exp(l2 - m)
        y = (w0 * o_ref[0, pair] + w1 * o_ref[1, pair] + w2 * o_ref[2, pair]) / (w0 + w1 + w2)
        y_ref[0, :, pair * LANES:(pair + 1) * LANES] = y.astype(y_ref.dtype)


def _attention(qkv1, qkv2, qkv3, *, ta=2048):
    bsz, s, _ = qkv1.shape
    nt = s // ta
    assert s % ta == 0 and ta % (ATTN_QB * DILATIONS[2]) == 0
    qkv1 = qkv1.reshape(bsz, 1, s, QKV_WIDTH)
    hb = ATTN_HALF
    specs = []
    for dil in DILATIONS:
        n = ta // dil
        last = s // dil // hb - 1
        specs += [
            pl.BlockSpec((1, dil, n, QKV_WIDTH), lambda b, i: (b, 0, i, 0)),
            pl.BlockSpec((1, dil, hb, QKV_WIDTH), lambda b, i, n=n: (b, 0, jnp.maximum(i * (n // hb) - 1, 0), 0)),
            pl.BlockSpec((1, dil, hb, QKV_WIDTH),
                         lambda b, i, n=n, last=last: (b, 0, jnp.minimum((i + 1) * (n // hb), last), 0)),
        ]
    bias = _attn_bias_table()
    kvw = 2 * PATTERN_WIDTH
    return pl.pallas_call(
        functools.partial(_attn_kernel, ta=ta, nt=nt, seq=s),
        grid=(bsz, nt),
        in_specs=[pl.BlockSpec(bias.shape, lambda b, i: (0, 0, 0))] + specs,
        out_specs=pl.BlockSpec((1, ta, PATTERN_WIDTH), lambda b, i: (b, i, 0)),
        out_shape=jax.ShapeDtypeStruct((bsz, s, PATTERN_WIDTH), jnp.bfloat16),
        scratch_shapes=[pltpu.VMEM((dil, ta // dil + 2 * hb, kvw), jnp.bfloat16) for dil in DILATIONS] + [
            pltpu.VMEM((len(DILATIONS), PATTERN_WIDTH // LANES, ta, LANES), jnp.float32),
            pltpu.VMEM((len(DILATIONS), PATTERN_WIDTH // LANES, ta, LANES), jnp.float32),
        ],
        compiler_params=pltpu.CompilerParams(
            dimension_semantics=("arbitrary", "arbitrary"), vmem_limit_bytes=V7X_VMEM_LIMIT),
        name="dilated_attn",
    )(bias, qkv1, qkv1, qkv1, qkv2, qkv2, qkv2, qkv3, qkv3, qkv3)


def _layer_norm(v, g, b):
    mu = jnp.mean(v, axis=-1, keepdims=True)
    c = v - mu
    var = jnp.mean(c * c, axis=-1, keepdims=True)
    return c * lax.rsqrt(var + NORM_EPS) * g + b


def _merge_mlp_kernel(x_ref, yssd_ref, yatt_ref, gate_ref, bgate_ref, wps_ref, wpa_ref, wout_ref,
                      ln1g_ref, ln1b_ref, wup_ref, wdown_ref, ln2g_ref, ln2b_ref, o_ref, *, ffc):
    f32, bf16 = jnp.float32, jnp.bfloat16
    ys = jnp.dot(yssd_ref[0], wps_ref[...], preferred_element_type=f32)
    ya = jnp.dot(yatt_ref[0], wpa_ref[...], preferred_element_type=f32)
    g_ssd = _sigmoid(gate_ref[0, :, 0:D_MODEL].astype(f32) + bgate_ref[:, 0:D_MODEL])
    g_att = _sigmoid(gate_ref[0, :, D_MODEL:2 * D_MODEL].astype(f32) + bgate_ref[:, D_MODEL:2 * D_MODEL])
    mix = jnp.dot((g_ssd * ys + g_att * ya).astype(bf16), wout_ref[...], preferred_element_type=f32)
    h = _layer_norm(ALPHA * x_ref[0] + mix, ln1g_ref[...], ln1b_ref[...])
    hb = h.astype(bf16)
    f = jnp.zeros_like(h)
    for c in range(0, D_FF, ffc):
        up = jnp.maximum(jnp.dot(hb, wup_ref[:, c:c + ffc], preferred_element_type=f32), 0.0)
        f = f + jnp.dot((up * up).astype(bf16), wdown_ref[c:c + ffc, :], preferred_element_type=f32)
    o_ref[0] = _layer_norm(ALPHA * h + f, ln2g_ref[...], ln2b_ref[...]).astype(o_ref.dtype)


def _merge_mlp(x, y_ssd, y_att, gate, b_gate, w_proj_ssd, w_proj_attn, w_out, ln1_g, ln1_b, w_up, w_down,
               ln2_g, ln2_b, *, tm=512, ffc=1024):
    bsz, s, d = x.shape
    assert s % tm == 0 and D_FF % ffc == 0
    bf16, f32 = jnp.bfloat16, jnp.float32
    row = lambda v: v.astype(f32).reshape(1, -1)
    tok = lambda width: pl.BlockSpec((1, tm, width), lambda b, i: (b, i, 0))
    resident = lambda shape: pl.BlockSpec(shape, lambda b, i: (0,) * len(shape), pipeline_mode=pl.Buffered(1))
    operands = [
        (x, tok(d)), (y_ssd, tok(D_INNER)), (y_att, tok(PATTERN_WIDTH)), (gate, tok(2 * D_MODEL)),
        (row(b_gate), None), (w_proj_ssd.astype(bf16), None), (w_proj_attn.astype(bf16), None),
        (w_out.astype(bf16), None), (row(ln1_g), None), (row(ln1_b), None),
        (w_up.astype(bf16), None), (w_down.astype(bf16), None), (row(ln2_g), None), (row(ln2_b), None),
    ]
    args = [a for a, _ in operands]
    in_specs = [spec if spec is not None else resident(a.shape) for a, spec in operands]
    return pl.pallas_call(
        functools.partial(_merge_mlp_kernel, ffc=ffc),
        grid=(bsz, s // tm),
        in_specs=in_specs,
        out_specs=tok(d),
        out_shape=jax.ShapeDtypeStruct((bsz, s, d), x.dtype),
        compiler_params=pltpu.CompilerParams(
            dimension_semantics=("arbitrary", "arbitrary"), vmem_limit_bytes=V7X_VMEM_LIMIT),
        name="merge_mlp",
    )(*args)


def kernel(x, w_in, b_gate, conv_w, conv_b, dt_bias_f, dt_bias_b, a_log_f, a_log_b, d_skip, ssd_norm_w,
           w_proj_ssd, w_proj_attn, w_out, ln1_g, ln1_b, w_up, w_down, ln2_g, ln2_b):
    w_packed, wdt_t = _pack_w_in(w_in)
    z, xbc, qkv1, qkv2, qkv3, gate, dtt = _in_proj(x, w_packed, wdt_t)
    y_ssd = _ssd(xbc, dtt, z, conv_w, conv_b, dt_bias_f, dt_bias_b, a_log_f, a_log_b, d_skip, ssd_norm_w)
    y_att = _attention(qkv1, qkv2, qkv3)
    return _merge_mlp(x, y_ssd, y_att, gate, b_gate, w_proj_ssd, w_proj_attn, w_out, ln1_g, ln1_b,
                      w_up, w_down, ln2_g, ln2_b)
```

```python
import functools
import math

import jax
import jax.numpy as jnp
import numpy as np
from jax import lax
from jax.experimental import pallas as pl
from jax.experimental.pallas import tpu as pltpu

D_MODEL = 1024
D_INNER = 2048
SSD_HEAD_DIM = 64
SSD_GROUPS = 4
SSD_HEADS_PER_GROUP = 8
D_STATE = 128
D_CONV = 5
SSD_CHUNK = 128
GROUP_X = SSD_HEADS_PER_GROUP * SSD_HEAD_DIM
GROUP_XBC = GROUP_X + 2 * D_STATE
CONV_DIM = D_INNER + 2 * SSD_GROUPS * D_STATE
NORM_EPS = 1e-5
ATTN_HEAD_DIM = 64
DILATIONS = (1, 4, 16)
ATTN_HALF = 64
HEADS_PER_PATTERN = 4
PATTERN_WIDTH = HEADS_PER_PATTERN * ATTN_HEAD_DIM
QKV_WIDTH = 3 * PATTERN_WIDTH
ATTN_WIDTH = 3 * PATTERN_WIDTH
D_FF = 4 * D_MODEL
DT_PER_GROUP = 2 * SSD_HEADS_PER_GROUP
DT_ROWS = SSD_GROUPS * DT_PER_GROUP
ALPHA = 2.0 ** 0.25
NEG_BIG = -1e30
LOG2E = 1.4426950408889634

V7X_VMEM_LIMIT = 56 * 1024 * 1024
LANES = 128
SUBLANES = 8
BF16_SUBLANES = 16
CONV_ROW_STRIDE = 4

_OFF_Z = 0
_OFF_XBC = _OFF_Z + D_INNER
_OFF_QKV = _OFF_XBC + CONV_DIM
_OFF_GATE = _OFF_QKV + 3 * QKV_WIDTH
IN_COLS_PACKED = _OFF_GATE + 2 * D_MODEL


def _group_xbc(a):
    nb = SSD_GROUPS * D_STATE
    xs, bs, cs = a[..., :D_INNER], a[..., D_INNER:D_INNER + nb], a[..., D_INNER + nb:]
    parts = []
    for g in range(SSD_GROUPS):
        parts += [xs[..., g * GROUP_X:(g + 1) * GROUP_X], bs[..., g * D_STATE:(g + 1) * D_STATE],
                  cs[..., g * D_STATE:(g + 1) * D_STATE]]
    return jnp.concatenate(parts, axis=-1)


def _pack_w_in(w_in):
    splits = np.cumsum([D_INNER, CONV_DIM, 32, 32, ATTN_WIDTH, ATTN_WIDTH, ATTN_WIDTH])
    wz, wxbc, wdtf, wdtb, wq, wk, wv, wg = jnp.split(w_in, list(splits), axis=1)
    wxbc = _group_xbc(wxbc)
    wq = wq * (1.0 / math.sqrt(ATTN_HEAD_DIM))
    qkv = []
    for p in range(3):
        s = slice(p * PATTERN_WIDTH, (p + 1) * PATTERN_WIDTH)
        qkv += [wq[:, s], wk[:, s], wv[:, s]]
    dts = []
    for g in range(SSD_GROUPS):
        s = slice(g * SSD_HEADS_PER_GROUP, (g + 1) * SSD_HEADS_PER_GROUP)
        dts += [wdtf[:, s], wdtb[:, s]]
    packed = jnp.concatenate([wz, wxbc] + qkv + [wg], axis=1).astype(jnp.bfloat16)
    wdt_t = jnp.concatenate(dts, axis=1).T.astype(jnp.bfloat16)
    return packed, wdt_t


def _sigmoid(v):
    return 0.5 * (1.0 + jnp.tanh(0.5 * v))


def _in_proj_kernel(x_ref, xprev_ref, xnext_ref, w_ref, wdt_t_ref, convw_ref, convb_ref,
                    z_ref, xbc_ref, qkv1_ref, qkv2_ref, qkv3_ref, gate_ref, dtt_ref,
                    xb_ref, cin_ref, cout_ref, stage_ref, *, tm, nc):
    i = pl.program_id(1)
    f32, bf16 = jnp.float32, jnp.bfloat16
    halo = BF16_SUBLANES
    pad = D_CONV // 2
    xb_ref[0:halo, :] = jnp.where(i > 0, xprev_ref[0], 0.0).astype(bf16)
    xb_ref[halo:halo + tm, :] = x_ref[0].astype(bf16)
    xb_ref[halo + tm:2 * halo + tm, :] = jnp.where(i < pl.num_programs(1) - 1, xnext_ref[0], 0.0).astype(bf16)

    def proj(off, width):
        return jnp.dot(xb_ref[halo:halo + tm, :], w_ref[:, off:off + width], preferred_element_type=f32)

    nslab = nc // LANES

    def store(out_ref, off, c):
        def run():
            out_ref[0, :, c:c + nc] = proj(off + c, nc).astype(out_ref.dtype)
        return run

    def qkv_dilated(out_ref, pat, dil, u):
        def run():
            res = proj(_OFF_QKV + pat * QKV_WIDTH + u * nc, nc)
            for c in range(nslab):
                stage_ref[u * nslab + c] = res[:, c * LANES:(c + 1) * LANES]
            for r in range(dil):
                for c in range(nslab):
                    out_ref[0, r, :, u * nc + c * LANES:u * nc + (c + 1) * LANES] = (
                        stage_ref[u * nslab + c, pl.ds(r, tm // dil, stride=dil), :].astype(out_ref.dtype))
        return run

    def dt_rows():
        dtt = lax.dot_general(wdt_t_ref[...], xb_ref[halo:halo + tm, :], (((1,), (1,)), ((), ())),
                              preferred_element_type=f32)
        for g in range(SSD_GROUPS):
            dtt_ref[0, g] = dtt[g * DT_PER_GROUP:(g + 1) * DT_PER_GROUP, :]

    def conv_slab(buf, s, cs):
        taps = [jnp.broadcast_to(convw_ref[k:k + 1, cs], (SUBLANES, LANES)) for k in range(D_CONV)]
        bias = jnp.broadcast_to(convb_ref[:, cs], (SUBLANES, LANES))
        for r0 in range(0, tm, SUBLANES * CONV_ROW_STRIDE):
            for v in range(CONV_ROW_STRIDE):
                acc = bias
                for k in range(D_CONV):
                    lo = halo - pad + k + r0 + v
                    acc = acc + taps[k] * cin_ref[buf, s, pl.ds(lo, SUBLANES, stride=CONV_ROW_STRIDE), :]
                cout_ref[buf, s, pl.ds(r0 + v, SUBLANES, stride=CONV_ROW_STRIDE), :] = acc * _sigmoid(acc)
        xbc_ref[0, :, cs] = cout_ref[buf, s].astype(xbc_ref.dtype)

    others = ([store(z_ref, _OFF_Z, c) for c in range(0, D_INNER, nc)]
              + [store(gate_ref, _OFF_GATE, c) for c in range(0, 2 * D_MODEL, nc)]
              + [store(qkv1_ref, _OFF_QKV, c) for c in range(0, QKV_WIDTH, nc)]
              + [qkv_dilated(qkv2_ref, 1, DILATIONS[1], u) for u in range(QKV_WIDTH // nc)]
              + [qkv_dilated(qkv3_ref, 2, DILATIONS[2], u) for u in range(QKV_WIDTH // nc)] + [dt_rows])
    nchunk = CONV_DIM // nc
    for c in range(nchunk):
        buf = c % 2
        ext = jnp.dot(xb_ref[...], w_ref[:, _OFF_XBC + c * nc:_OFF_XBC + (c + 1) * nc],
                      preferred_element_type=f32)
        for s in range(nslab):
            cin_ref[buf, s] = ext[:, s * LANES:(s + 1) * LANES]
        for s in range(nslab):
            if others:
                others.pop(0)()
            conv_slab(buf, s, slice(c * nc + s * LANES, c * nc + (s + 1) * LANES))
    for job in others:
        job()


def _in_proj(x, w_packed, wdt_t, conv_w, conv_b, *, tm=512, nc=256):
    bsz, s, d = x.shape
    assert s % tm == 0 and tm % (BF16_SUBLANES * DILATIONS[2]) == 0 and CONV_DIM % nc == 0
    act = jnp.bfloat16
    halo = BF16_SUBLANES
    hb = tm // halo
    convw = _group_xbc(conv_w.astype(jnp.float32))
    convb = _group_xbc(conv_b.astype(jnp.float32)).reshape(1, CONV_DIM)
    out_shape = (
        jax.ShapeDtypeStruct((bsz, s, D_INNER), act),
        jax.ShapeDtypeStruct((bsz, s, CONV_DIM), act),
        jax.ShapeDtypeStruct((bsz, s, QKV_WIDTH), act),
        jax.ShapeDtypeStruct((bsz, DILATIONS[1], s // DILATIONS[1], QKV_WIDTH), act),
        jax.ShapeDtypeStruct((bsz, DILATIONS[2], s // DILATIONS[2], QKV_WIDTH), act),
        jax.ShapeDtypeStruct((bsz, s, 2 * D_MODEL), act),
        jax.ShapeDtypeStruct((bsz, SSD_GROUPS, DT_PER_GROUP, s), jnp.float32),
    )
    tok = lambda width: pl.BlockSpec((1, tm, width), lambda b, i: (b, i, 0))
    resident = lambda shape: pl.BlockSpec(shape, lambda b, i: (0,) * len(shape),
                                          pipeline_mode=pl.Buffered(1))
    return pl.pallas_call(
        functools.partial(_in_proj_kernel, tm=tm, nc=nc),
        grid=(bsz, s // tm),
        in_specs=[
            tok(d),
            pl.BlockSpec((1, halo, d), lambda b, i: (b, jnp.maximum(i * hb - 1, 0), 0)),
            pl.BlockSpec((1, halo, d), lambda b, i: (b, jnp.minimum((i + 1) * hb, s // halo - 1), 0)),
            resident(w_packed.shape), resident(wdt_t.shape), resident(convw.shape), resident(convb.shape),
        ],
        out_specs=(
            tok(D_INNER), tok(CONV_DIM), tok(QKV_WIDTH),
            pl.BlockSpec((1, DILATIONS[1], tm // DILATIONS[1], QKV_WIDTH), lambda b, i: (b, 0, i, 0)),
            pl.BlockSpec((1, DILATIONS[2], tm // DILATIONS[2], QKV_WIDTH), lambda b, i: (b, 0, i, 0)),
            tok(2 * D_MODEL),
            pl.BlockSpec((1, SSD_GROUPS, DT_PER_GROUP, tm), lambda b, i: (b, 0, 0, i)),
        ),
        out_shape=out_shape,
        scratch_shapes=[
            pltpu.VMEM((tm + 2 * halo, d), jnp.bfloat16),
            pltpu.VMEM((2, nc // LANES, tm + 2 * halo, LANES), jnp.float32),
            pltpu.VMEM((2, nc // LANES, tm, LANES), jnp.float32),
            pltpu.VMEM((QKV_WIDTH // LANES, tm, LANES), jnp.float32),
        ],
        compiler_params=pltpu.CompilerParams(
            dimension_semantics=("arbitrary", "arbitrary"), vmem_limit_bytes=V7X_VMEM_LIMIT),
        name="in_proj",
    )(x, x, x, w_packed, wdt_t, convw, convb)


def _split3_bf16(a):
    hi = a.astype(jnp.bfloat16)
    r1 = a - hi.astype(jnp.float32)
    mid = r1.astype(jnp.bfloat16)
    lo = (r1 - mid.astype(jnp.float32)).astype(jnp.bfloat16)
    return hi, mid, lo


_ROW_CUM, _ROW_E, _ROW_W = 32, 56, 80
_STACK_PARTS = 3
_BCAST_COLS = SSD_HEADS_PER_GROUP * SSD_CHUNK + 2 * GROUP_X
LOG2_DT_FLOOR = -1e4


def _ssd_select_matrix():
    r = np.zeros((SSD_CHUNK, _BCAST_COLS), np.float32)
    hd, q = SSD_HEADS_PER_GROUP, SSD_CHUNK
    for h in range(hd):
        for part in range(_STACK_PARTS):
            r[_ROW_CUM + part * hd + h, h * q:(h + 1) * q] = 1.0
            e0 = hd * q + h * SSD_HEAD_DIM
            r[_ROW_E + part * hd + h, e0:e0 + SSD_HEAD_DIM] = 1.0
            r[_ROW_W + part * hd + h, e0 + GROUP_X:e0 + GROUP_X + SSD_HEAD_DIM] = 1.0
    return jnp.asarray(r, dtype=jnp.bfloat16)


def _ssd_blockdiag_mask():
    hd, q = SSD_HEADS_PER_GROUP, SSD_CHUNK
    m = np.zeros((_ROW_CUM, hd * q), np.float32)
    for part in range(_STACK_PARTS):
        for h in range(hd):
            m[part * hd + h, h * q:(h + 1) * q] = 1.0
    return jnp.asarray(m)


def _ssd_decay_terms(dt_raw_t, dt_bias_c, a_log_c, bdmask_ref, *, backward):
    q, hd = SSD_CHUNK, SSD_HEADS_PER_GROUP
    f32, bf16 = jnp.float32, jnp.bfloat16
    row = lax.broadcasted_iota(jnp.int32, (q, q), 0)
    col = lax.broadcasted_iota(jnp.int32, (q, q), 1)
    tri = ((row >= col) if backward else (row <= col)).astype(f32).astype(bf16)
    end = 0 if backward else q - 1
    v = dt_raw_t + dt_bias_c
    dt = jnp.maximum(v, 0.0) + jnp.log1p(jnp.exp(-jnp.abs(v)))
    a2 = dt * (-jnp.exp(a_log_c) * LOG2E)
    cum3 = jnp.dot(jnp.concatenate(_split3_bf16(a2), axis=0), tri, preferred_element_type=f32)
    cum2 = cum3[0:hd] + cum3[hd:2 * hd] + cum3[2 * hd:3 * hd]
    tot2 = cum2[:, end:end + 1]
    e_r = jnp.exp2(cum2)
    w_r = dt * jnp.exp2(tot2 - cum2)
    nrowp = jnp.maximum(jnp.log2(dt), LOG2_DT_FLOOR) - cum2
    stacked = jnp.concatenate((jnp.ones((_ROW_CUM, q), bf16),) + _split3_bf16(cum2) + _split3_bf16(e_r)
                              + _split3_bf16(w_r) + (jnp.zeros((q - _ROW_W - _STACK_PARTS * hd, q), bf16),),
                              axis=0)
    cols = stacked.astype(f32).T.astype(bf16)
    nparts = jnp.concatenate(_split3_bf16(nrowp) + (jnp.zeros((hd, q), bf16),), axis=0).astype(f32)
    data_rows = (jnp.tile(nparts, (1, hd)) * bdmask_ref[...]).astype(bf16)
    return cols, data_rows


def _ssd_intra(xq, bq, cq, big_diff, w_exp, *, backward):
    q, hd = SSD_CHUNK, SSD_HEADS_PER_GROUP
    f32, bf16 = jnp.float32, jnp.bfloat16
    row = lax.broadcasted_iota(jnp.int32, (q, q), 0)
    col = lax.broadcasted_iota(jnp.int32, (q, q), 1)
    mask = (row <= col) if backward else (row >= col)
    cb = lax.dot_general(cq, bq, (((1,), (1,)), ((), ())), preferred_element_type=f32)
    lane = lax.broadcasted_iota(jnp.int32, (q, LANES), 1)
    first_half = lane < SSD_HEAD_DIM
    ys = []
    for j in range(hd // 2):
        xpair = xq[:, j * LANES:(j + 1) * LANES]
        ms = []
        for h in (2 * j, 2 * j + 1):
            diff = big_diff[:, h * q:(h + 1) * q]
            ms.append((cb * jnp.exp2(jnp.where(mask, diff, NEG_BIG))).astype(bf16))
        lhs = jnp.concatenate(ms, axis=1)
        zero = jnp.zeros_like(xpair)
        rhs = jnp.concatenate([jnp.where(first_half, xpair, zero),
                               jnp.where(first_half, zero, xpair)], axis=0)
        ys.append(jnp.dot(lhs, rhs, preferred_element_type=f32))
    xw = xq * w_exp.astype(bf16)
    s_new = lax.dot_general(bq, xw, (((0,), (0,)), ((), ())), preferred_element_type=f32)
    return jnp.concatenate(ys, axis=1), s_new


def _ssd_kernel(xc_ref, dtt_ref, z_ref, dtb_ref, alog_ref, dskip_ref, normw_ref, sel_ref, bdmask_ref, y_ref,
                yf_ref, h_ref, ew_ref, *, lb, nb):
    ph = pl.program_id(2)
    i = pl.program_id(3)
    q = SSD_CHUNK
    hd = SSD_HEADS_PER_GROUP
    nck = lb // q
    f32, bf16 = jnp.float32, jnp.bfloat16
    blk = lambda c: slice(c * q, (c + 1) * q)

    def scan_block(*, backward, finish):
        hs = slice(hd, DT_PER_GROUP) if backward else slice(0, hd)
        order = list(reversed(range(nck))) if backward else list(range(nck))
        end = 0 if backward else q - 1
        terms = [_ssd_decay_terms(dtt_ref[0, 0, hs, blk(c)], dtb_ref[0, hs, :], alog_ref[0, hs, :],
                                  bdmask_ref, backward=backward) for c in range(nck)]

        def broadcasts(c):
            cols, data_rows = terms[c]
            ew_ref[blk(c), :] = jnp.dot(cols, sel_ref[:, hd * q:], preferred_element_type=f32)
            r_diff = jnp.concatenate([data_rows, sel_ref[_ROW_CUM:, 0:hd * q]], axis=0)
            return jnp.dot(cols, r_diff, preferred_element_type=f32)

        h = h_ref[...]
        big_diff = broadcasts(order[0])
        for n, c in enumerate(order):
            next_diff = broadcasts(order[n + 1]) if n + 1 < nck else None
            cq = xc_ref[0, blk(c), GROUP_X + D_STATE:GROUP_XBC]
            yd, sn = _ssd_intra(xc_ref[0, blk(c), 0:GROUP_X], xc_ref[0, blk(c), GROUP_X:GROUP_X + D_STATE], cq,
                                big_diff, ew_ref[blk(c), GROUP_X:], backward=backward)
            ch = jnp.dot(cq, h.astype(bf16), preferred_element_type=f32)
            finish(c, yd + ew_ref[blk(c), 0:GROUP_X] * ch)
            decay = ew_ref[c * q + end:c * q + end + 1, 0:GROUP_X]
            h = decay * h + sn
            big_diff = next_diff
        h_ref[...] = h

    @pl.when(i == 0)
    def _():
        h_ref[...] = jnp.zeros_like(h_ref)

    @pl.when(ph == 0)
    def _forward():
        base = pl.multiple_of(i * lb, lb)

        def finish(c, y):
            yf_ref[pl.ds(base + c * q, q), :] = y

        scan_block(backward=False, finish=finish)

    @pl.when(ph == 1)
    def _backward():
        base = pl.multiple_of((nb - 1 - i) * lb, lb)

        def finish(c, y):
            y = y + yf_ref[pl.ds(base + c * q, q), :] + dskip_ref[0] * xc_ref[0, blk(c), 0:GROUP_X].astype(f32)
            zq = z_ref[0, blk(c), :].astype(f32)
            y = y * (zq * _sigmoid(zq))
            ms = jnp.mean(y * y, axis=-1, keepdims=True)
            y_ref[0, blk(c), :] = (y * lax.rsqrt(ms + NORM_EPS) * normw_ref[0]).astype(y_ref.dtype)

        scan_block(backward=True, finish=finish)


def _ssd(xc, dtt, z, dt_bias_f, dt_bias_b, a_log_f, a_log_b, d_skip, ssd_norm_w, *, lb=1024):
    bsz, s, _ = xc.shape
    nb = s // lb
    assert s % lb == 0 and lb % SSD_CHUNK == 0
    f32 = jnp.float32
    per_group = lambda f, b: jnp.concatenate(
        [f.astype(f32).reshape(SSD_GROUPS, SSD_HEADS_PER_GROUP), b.astype(f32).reshape(SSD_GROUPS, SSD_HEADS_PER_GROUP)],
        axis=1).reshape(SSD_GROUPS, DT_PER_GROUP, 1)
    dtb = per_group(dt_bias_f, dt_bias_b)
    alog = per_group(a_log_f, a_log_b)
    dskip = jnp.repeat(d_skip.astype(f32), SSD_HEAD_DIM).reshape(SSD_GROUPS, 1, GROUP_X)
    normw = ssd_norm_w.astype(f32).reshape(SSD_GROUPS, 1, GROUP_X)
    sel, bdmask = _ssd_select_matrix(), _ssd_blockdiag_mask()

    any_blk = lambda ph, i: jnp.where(ph == 0, i, nb - 1 - i)
    bwd_blk = lambda ph, i: jnp.where(ph == 0, nb - 1, nb - 1 - i)
    par = lambda shape: pl.BlockSpec(shape, lambda b, g, ph, i: (g, 0, 0))
    return pl.pallas_call(
        functools.partial(_ssd_kernel, lb=lb, nb=nb),
        grid=(bsz, SSD_GROUPS, 2, nb),
        in_specs=[
            pl.BlockSpec((1, lb, GROUP_XBC), lambda b, g, ph, i: (b, any_blk(ph, i), g)),
            pl.BlockSpec((1, 1, DT_PER_GROUP, lb), lambda b, g, ph, i: (b, g, 0, any_blk(ph, i))),
            pl.BlockSpec((1, lb, GROUP_X), lambda b, g, ph, i: (b, bwd_blk(ph, i), g)),
            par((1, DT_PER_GROUP, 1)), par((1, DT_PER_GROUP, 1)),
            par((1, 1, GROUP_X)), par((1, 1, GROUP_X)),
            pl.BlockSpec(sel.shape, lambda b, g, ph, i: (0, 0)),
            pl.BlockSpec(bdmask.shape, lambda b, g, ph, i: (0, 0)),
        ],
        out_specs=pl.BlockSpec((1, lb, GROUP_X), lambda b, g, ph, i: (b, bwd_blk(ph, i), g)),
        out_shape=jax.ShapeDtypeStruct((bsz, s, D_INNER), jnp.bfloat16),
        scratch_shapes=[
            pltpu.VMEM((s, GROUP_X), f32),
            pltpu.VMEM((D_STATE, GROUP_X), f32),
            pltpu.VMEM((lb, 2 * GROUP_X), f32),
        ],
        compiler_params=pltpu.CompilerParams(
            dimension_semantics=("arbitrary",) * 4, vmem_limit_bytes=V7X_VMEM_LIMIT),
        name="ssd",
    )(xc, dtt, z, dtb, alog, dskip, normw, sel, bdmask)


ATTN_QB = 128
ATTN_WIN = ATTN_QB + 2 * ATTN_HALF
ATTN_BLOCKS_PER_ITER = 2


def _attn_bias_table():
    heads = len(DILATIONS) * HEADS_PER_PATTERN
    slopes = 2.0 ** (-8.0 * np.arange(1, heads + 1) / heads)
    rel = (np.arange(ATTN_WIN)[None, :] - ATTN_HALF) - np.arange(ATTN_QB)[:, None]
    tabs = []
    for h in range(heads):
        dil = DILATIONS[h // HEADS_PER_PATTERN]
        tabs.append(np.where(np.abs(rel) <= ATTN_HALF, -slopes[h] * np.abs(rel) * dil, NEG_BIG))
    return jnp.asarray(np.stack(tabs), dtype=jnp.float32)


def _attn_kernel(bias_ref, q1_ref, p1_ref, n1_ref, q2_ref, p2_ref, n2_ref, q3_ref, p3_ref, n3_ref, y_ref,
                 kv1_ref, kv2_ref, kv3_ref, o_ref, lse_ref, *, ta, nt, seq):
    i = pl.program_id(1)
    f32, bf16 = jnp.float32, jnp.bfloat16
    lane = lax.broadcasted_iota(jnp.int32, (ATTN_QB, LANES), 1)
    first_half = lane < ATTN_HEAD_DIM
    kcol = lax.broadcasted_iota(jnp.int32, (1, ATTN_WIN), 1)

    for pat, (dil, q_ref, prev_ref, next_ref, kv_ref) in enumerate((
            (DILATIONS[0], q1_ref, p1_ref, n1_ref, kv1_ref),
            (DILATIONS[1], q2_ref, p2_ref, n2_ref, kv2_ref),
            (DILATIONS[2], q3_ref, p3_ref, n3_ref, kv3_ref))):
        n = ta // dil
        seq_l = seq // dil
        kv_ref[:, 0:ATTN_HALF, :] = prev_ref[0, :, :, PATTERN_WIDTH:]
        kv_ref[:, ATTN_HALF:ATTN_HALF + n, :] = q_ref[0, :, :, PATTERN_WIDTH:]
        kv_ref[:, ATTN_HALF + n:2 * ATTN_HALF + n, :] = next_ref[0, :, :, PATTERN_WIDTH:]
        nqb = n // ATTN_QB

        def block(it, carry, dil=dil, q_ref=q_ref, kv_ref=kv_ref, nqb=nqb, n=n, seq_l=seq_l, pat=pat):
            nh = HEADS_PER_PATTERN
            items = range(ATTN_BLOCKS_PER_ITER * nh)
            places, pens, vps, scs = [], [], [], []
            for k in range(ATTN_BLOCKS_PER_ITER):
                idx = it * ATTN_BLOCKS_PER_ITER + k
                r = idx // nqb
                row0 = pl.multiple_of((idx % nqb) * ATTN_QB, ATTN_QB)
                places.append((r, row0))
                kpos = i * n + row0 - ATTN_HALF + kcol
                pens.append(jnp.where((kpos >= 0) & (kpos < seq_l), 0.0, NEG_BIG).astype(f32))
                for pair in range(nh // 2):
                    cs = slice(pair * LANES, (pair + 1) * LANES)
                    qp = q_ref[0, r, pl.ds(row0, ATTN_QB), cs]
                    kp = kv_ref[r, pl.ds(row0, ATTN_WIN), cs]
                    vps.append(kv_ref[r, pl.ds(row0, ATTN_WIN),
                                      pair * LANES + PATTERN_WIDTH:(pair + 1) * LANES + PATTERN_WIDTH])
                    zero = jnp.zeros_like(qp)
                    for qh in (jnp.where(first_half, qp, zero), jnp.where(first_half, zero, qp)):
                        scs.append(lax.dot_general(qh, kp, (((1,), (1,)), ((), ())), preferred_element_type=f32))
            scs = [scs[t] + bias_ref[pat * nh + t % nh] + pens[t // nh] for t in items]
            ms = [jnp.max(scs[t], axis=-1, keepdims=True) for t in items]
            pes = [jnp.exp(scs[t] - ms[t]) for t in items]
            dens = [jnp.sum(pes[t], axis=-1, keepdims=True) for t in items]
            pvs = [jnp.dot(pes[t].astype(bf16), vps[t // 2], preferred_element_type=f32) for t in items]
            outs = [pvs[t] / dens[t] for t in items]
            lses = [ms[t] + jnp.log(dens[t]) for t in items]
            for k, (r, row0) in enumerate(places):
                rows = pl.ds(r + row0 * dil, ATTN_QB, stride=dil)
                for pair in range(nh // 2):
                    t = k * nh + 2 * pair
                    o_ref[pat, pair, rows, :] = jnp.where(first_half, outs[t], outs[t + 1])
                    lse_ref[pat, pair, rows, :] = jnp.where(first_half, lses[t], lses[t + 1])
            return carry

        lax.fori_loop(0, dil * nqb // ATTN_BLOCKS_PER_ITER, block, 0)

    for pair in range(HEADS_PER_PATTERN // 2):
        l0, l1, l2 = lse_ref[0, pair], lse_ref[1, pair], lse_ref[2, pair]
        m = jnp.maximum(jnp.maximum(l0, l1), l2)
        w0, w1, w2 = jnp.exp(l0 - m), jnp.exp(l1 - m), jnp.exp(l2 - m)
        y = (w0 * o_ref[0, pair] + w1 * o_ref[1, pair] + w2 * o_ref[2, pair]) / (w0 + w1 + w2)
        y_ref[0, :, pair * LANES:(pair + 1) * LANES] = y.astype(y_ref.dtype)


def _attention(qkv1, qkv2, qkv3, *, ta=2048):
    bsz, s, _ = qkv1.shape
    nt = s // ta
    assert s % ta == 0 and ta % (ATTN_QB * DILATIONS[2]) == 0 and (ta // ATTN_QB) % ATTN_BLOCKS_PER_ITER == 0
    qkv1 = qkv1.reshape(bsz, 1, s, QKV_WIDTH)
    hb = ATTN_HALF
    specs = []
    for dil in DILATIONS:
        n = ta // dil
        last = s // dil // hb - 1
        specs += [
            pl.BlockSpec((1, dil, n, QKV_WIDTH), lambda b, i: (b, 0, i, 0)),
            pl.BlockSpec((1, dil, hb, QKV_WIDTH), lambda b, i, n=n: (b, 0, jnp.maximum(i * (n // hb) - 1, 0), 0)),
            pl.BlockSpec((1, dil, hb, QKV_WIDTH),
                         lambda b, i, n=n, last=last: (b, 0, jnp.minimum((i + 1) * (n // hb), last), 0)),
        ]
    bias = _attn_bias_table()
    kvw = 2 * PATTERN_WIDTH
    return pl.pallas_call(
        functools.partial(_attn_kernel, ta=ta, nt=nt, seq=s),
        grid=(bsz, nt),
        in_specs=[pl.BlockSpec(bias.shape, lambda b, i: (0, 0, 0))] + specs,
        out_specs=pl.BlockSpec((1, ta, PATTERN_WIDTH), lambda b, i: (b, i, 0)),
        out_shape=jax.ShapeDtypeStruct((bsz, s, PATTERN_WIDTH), jnp.bfloat16),
        scratch_shapes=[pltpu.VMEM((dil, ta // dil + 2 * hb, kvw), jnp.bfloat16) for dil in DILATIONS] + [
            pltpu.VMEM((len(DILATIONS), PATTERN_WIDTH // LANES, ta, LANES), jnp.float32),
            pltpu.VMEM((len(DILATIONS), PATTERN_WIDTH // LANES, ta, LANES), jnp.float32),
        ],
        compiler_params=pltpu.CompilerParams(
            dimension_semantics=("arbitrary", "arbitrary"), vmem_limit_bytes=V7X_VMEM_LIMIT),
        name="dilated_attn",
    )(bias, qkv1, qkv1, qkv1, qkv2, qkv2, qkv2, qkv3, qkv3, qkv3)


def _layer_norm(v, g, b):
    mu = jnp.mean(v, axis=-1, keepdims=True)
    c = v - mu
    var = jnp.mean(c * c, axis=-1, keepdims=True)
    return c * lax.rsqrt(var + NORM_EPS) * g + b


def _merge_mlp_kernel(x_ref, yssd_ref, yatt_ref, gate_ref, bgate_ref, wps_ref, wpa_ref, wout_ref,
                      ln1g_ref, ln1b_ref, wup_ref, wdown_ref, ln2g_ref, ln2b_ref, o_ref, *, ffc):
    f32, bf16 = jnp.float32, jnp.bfloat16
    ys = jnp.dot(yssd_ref[0], wps_ref[...], preferred_element_type=f32)
    ya = jnp.dot(yatt_ref[0], wpa_ref[...], preferred_element_type=f32)
    g_ssd = _sigmoid(gate_ref[0, :, 0:D_MODEL].astype(f32) + bgate_ref[:, 0:D_MODEL])
    g_att = _sigmoid(gate_ref[0, :, D_MODEL:2 * D_MODEL].astype(f32) + bgate_ref[:, D_MODEL:2 * D_MODEL])
    mix = jnp.dot((g_ssd * ys + g_att * ya).astype(bf16), wout_ref[...], preferred_element_type=f32)
    h = _layer_norm(ALPHA * x_ref[0] + mix, ln1g_ref[...], ln1b_ref[...])
    hb = h.astype(bf16)
    f = jnp.zeros_like(h)
    for c in range(0, D_FF, ffc):
        up = jnp.maximum(jnp.dot(hb, wup_ref[:, c:c + ffc], preferred_element_type=f32), 0.0)
        f = f + jnp.dot((up * up).astype(bf16), wdown_ref[c:c + ffc, :], preferred_element_type=f32)
    o_ref[0] = _layer_norm(ALPHA * h + f, ln2g_ref[...], ln2b_ref[...]).astype(o_ref.dtype)


def _merge_mlp(x, y_ssd, y_att, gate, b_gate, w_proj_ssd, w_proj_attn, w_out, ln1_g, ln1_b, w_up, w_down,
               ln2_g, ln2_b, *, tm=512, ffc=1024):
    bsz, s, d = x.shape
    assert s % tm == 0 and D_FF % ffc == 0
    bf16, f32 = jnp.bfloat16, jnp.float32
    row = lambda v: v.astype(f32).reshape(1, -1)
    tok = lambda width: pl.BlockSpec((1, tm, width), lambda b, i: (b, i, 0))
    resident = lambda shape: pl.BlockSpec(shape, lambda b, i: (0,) * len(shape), pipeline_mode=pl.Buffered(1))
    operands = [
        (x, tok(d)), (y_ssd, tok(D_INNER)), (y_att, tok(PATTERN_WIDTH)), (gate, tok(2 * D_MODEL)),
        (row(b_gate), None), (w_proj_ssd.astype(bf16), None), (w_proj_attn.astype(bf16), None),
        (w_out.astype(bf16), None), (row(ln1_g), None), (row(ln1_b), None),
        (w_up.astype(bf16), None), (w_down.astype(bf16), None), (row(ln2_g), None), (row(ln2_b), None),
    ]
    args = [a for a, _ in operands]
    in_specs = [spec if spec is not None else resident(a.shape) for a, spec in operands]
    return pl.pallas_call(
        functools.partial(_merge_mlp_kernel, ffc=ffc),
        grid=(bsz, s // tm),
        in_specs=in_specs,
        out_specs=tok(d),
        out_shape=jax.ShapeDtypeStruct((bsz, s, d), x.dtype),
        compiler_params=pltpu.CompilerParams(
            dimension_semantics=("arbitrary", "arbitrary"), vmem_limit_bytes=V7X_VMEM_LIMIT),
        name="merge_mlp",
    )(*args)


def kernel(x, w_in, b_gate, conv_w, conv_b, dt_bias_f, dt_bias_b, a_log_f, a_log_b, d_skip, ssd_norm_w,
           w_proj_ssd, w_proj_attn, w_out, ln1_g, ln1_b, w_up, w_down, ln2_g, ln2_b):
    w_packed, wdt_t = _pack_w_in(w_in)
    z, xc, qkv1, qkv2, qkv3, gate, dtt = _in_proj(x, w_packed, wdt_t, conv_w, conv_b)
    y_ssd = _ssd(xc, dtt, z, dt_bias_f, dt_bias_b, a_log_f, a_log_b, d_skip, ssd_norm_w)
    y_att = _attention(qkv1, qkv2, qkv3)
    return _merge_mlp(x, y_ssd, y_att, gate, b_gate, w_proj_ssd, w_proj_attn, w_out, ln1_g, ln1_b,
                      w_up, w_down, ln2_g, ln2_b)
```

```python
import functools
import math

import jax
import jax.numpy as jnp
import numpy as np
from jax import lax
from jax.experimental import pallas as pl
from jax.experimental.pallas import tpu as pltpu

D_MODEL = 1024
D_INNER = 2048
SSD_HEAD_DIM = 64
SSD_GROUPS = 4
SSD_HEADS_PER_GROUP = 8
D_STATE = 128
D_CONV = 5
SSD_CHUNK = 128
GROUP_X = SSD_HEADS_PER_GROUP * SSD_HEAD_DIM
GROUP_XBC = GROUP_X + 2 * D_STATE
CONV_DIM = D_INNER + 2 * SSD_GROUPS * D_STATE
NORM_EPS = 1e-5
ATTN_HEAD_DIM = 64
DILATIONS = (1, 4, 16)
ATTN_HALF = 64
HEADS_PER_PATTERN = 4
PATTERN_WIDTH = HEADS_PER_PATTERN * ATTN_HEAD_DIM
QKV_WIDTH = 3 * PATTERN_WIDTH
ATTN_WIDTH = 3 * PATTERN_WIDTH
D_FF = 4 * D_MODEL
DT_PER_GROUP = 2 * SSD_HEADS_PER_GROUP
DT_ROWS = SSD_GROUPS * DT_PER_GROUP
ALPHA = 2.0 ** 0.25
NEG_BIG = -1e30
LOG2E = 1.4426950408889634

V7X_VMEM_LIMIT = 56 * 1024 * 1024
LANES = 128
SUBLANES = 8
BF16_SUBLANES = 16
CONV_ROW_STRIDE = 4

_OFF_Z = 0
_OFF_XBC = _OFF_Z + D_INNER
_OFF_QKV = _OFF_XBC + CONV_DIM
_OFF_GATE = _OFF_QKV + 3 * QKV_WIDTH
IN_COLS_PACKED = _OFF_GATE + 2 * D_MODEL


def _group_xbc(a):
    nb = SSD_GROUPS * D_STATE
    xs, bs, cs = a[..., :D_INNER], a[..., D_INNER:D_INNER + nb], a[..., D_INNER + nb:]
    parts = []
    for g in range(SSD_GROUPS):
        parts += [xs[..., g * GROUP_X:(g + 1) * GROUP_X], bs[..., g * D_STATE:(g + 1) * D_STATE],
                  cs[..., g * D_STATE:(g + 1) * D_STATE]]
    return jnp.concatenate(parts, axis=-1)


def _pack_w_in(w_in):
    splits = np.cumsum([D_INNER, CONV_DIM, 32, 32, ATTN_WIDTH, ATTN_WIDTH, ATTN_WIDTH])
    wz, wxbc, wdtf, wdtb, wq, wk, wv, wg = jnp.split(w_in, list(splits), axis=1)
    wxbc = _group_xbc(wxbc)
    wq = wq * (1.0 / math.sqrt(ATTN_HEAD_DIM))
    qkv = []
    for p in range(3):
        s = slice(p * PATTERN_WIDTH, (p + 1) * PATTERN_WIDTH)
        qkv += [wq[:, s], wk[:, s], wv[:, s]]
    dts = []
    for g in range(SSD_GROUPS):
        s = slice(g * SSD_HEADS_PER_GROUP, (g + 1) * SSD_HEADS_PER_GROUP)
        dts += [wdtf[:, s], wdtb[:, s]]
    packed = jnp.concatenate([wz, wxbc] + qkv + [wg], axis=1).astype(jnp.bfloat16)
    wdt_t = jnp.concatenate(dts, axis=1).T.astype(jnp.bfloat16)
    return packed, wdt_t


def _sigmoid(v):
    return 0.5 * (1.0 + jnp.tanh(0.5 * v))


def _in_proj_kernel(x_ref, xprev_ref, xnext_ref, w_ref, wdt_t_ref, convw_ref, convb_ref,
                    z_ref, xbc_ref, qkv1_ref, qkv2_ref, qkv3_ref, gate_ref, dtt_ref,
                    xb_ref, cin_ref, cout_ref, stage_ref, *, tm, nc):
    i = pl.program_id(1)
    f32, bf16 = jnp.float32, jnp.bfloat16
    halo = BF16_SUBLANES
    pad = D_CONV // 2
    xb_ref[0:halo, :] = jnp.where(i > 0, xprev_ref[0], 0.0).astype(bf16)
    xb_ref[halo:halo + tm, :] = x_ref[0].astype(bf16)
    xb_ref[halo + tm:2 * halo + tm, :] = jnp.where(i < pl.num_programs(1) - 1, xnext_ref[0], 0.0).astype(bf16)

    def proj(off, width):
        return jnp.dot(xb_ref[halo:halo + tm, :], w_ref[:, off:off + width], preferred_element_type=f32)

    nslab = nc // LANES

    def store(out_ref, off, c):
        def run():
            out_ref[0, :, c:c + nc] = proj(off + c, nc).astype(out_ref.dtype)
        return run

    def qkv_dilated(out_ref, pat, dil, u):
        def run():
            res = proj(_OFF_QKV + pat * QKV_WIDTH + u * nc, nc)
            for c in range(nslab):
                stage_ref[u * nslab + c] = res[:, c * LANES:(c + 1) * LANES]
            for r in range(dil):
                for c in range(nslab):
                    out_ref[0, r, :, u * nc + c * LANES:u * nc + (c + 1) * LANES] = (
                        stage_ref[u * nslab + c, pl.ds(r, tm // dil, stride=dil), :].astype(out_ref.dtype))
        return run

    def dt_rows():
        dtt = lax.dot_general(wdt_t_ref[...], xb_ref[halo:halo + tm, :], (((1,), (1,)), ((), ())),
                              preferred_element_type=f32)
        for g in range(SSD_GROUPS):
            dtt_ref[0, g] = dtt[g * DT_PER_GROUP:(g + 1) * DT_PER_GROUP, :]

    def conv_slab(buf, s, cs):
        taps = [jnp.broadcast_to(convw_ref[k:k + 1, cs], (SUBLANES, LANES)) for k in range(D_CONV)]
        bias = jnp.broadcast_to(convb_ref[:, cs], (SUBLANES, LANES))
        for r0 in range(0, tm, SUBLANES * CONV_ROW_STRIDE):
            for v in range(CONV_ROW_STRIDE):
                acc = bias
                for k in range(D_CONV):
                    lo = halo - pad + k + r0 + v
                    acc = acc + taps[k] * cin_ref[buf, s, pl.ds(lo, SUBLANES, stride=CONV_ROW_STRIDE), :]
                cout_ref[buf, s, pl.ds(r0 + v, SUBLANES, stride=CONV_ROW_STRIDE), :] = acc * _sigmoid(acc)
        xbc_ref[0, :, cs] = cout_ref[buf, s].astype(xbc_ref.dtype)

    others = ([store(z_ref, _OFF_Z, c) for c in range(0, D_INNER, nc)]
              + [store(gate_ref, _OFF_GATE, c) for c in range(0, 2 * D_MODEL, nc)]
              + [store(qkv1_ref, _OFF_QKV, c) for c in range(0, QKV_WIDTH, nc)]
              + [qkv_dilated(qkv2_ref, 1, DILATIONS[1], u) for u in range(QKV_WIDTH // nc)]
              + [qkv_dilated(qkv3_ref, 2, DILATIONS[2], u) for u in range(QKV_WIDTH // nc)] + [dt_rows])
    nchunk = CONV_DIM // nc
    for c in range(nchunk):
        buf = c % 2
        ext = jnp.dot(xb_ref[...], w_ref[:, _OFF_XBC + c * nc:_OFF_XBC + (c + 1) * nc],
                      preferred_element_type=f32)
        for s in range(nslab):
            cin_ref[buf, s] = ext[:, s * LANES:(s + 1) * LANES]
        for s in range(nslab):
            if others:
                others.pop(0)()
            conv_slab(buf, s, slice(c * nc + s * LANES, c * nc + (s + 1) * LANES))
    for job in others:
        job()


def _in_proj(x, w_packed, wdt_t, conv_w, conv_b, *, tm=512, nc=256):
    bsz, s, d = x.shape
    assert s % tm == 0 and tm % (BF16_SUBLANES * DILATIONS[2]) == 0 and CONV_DIM % nc == 0
    act = jnp.bfloat16
    halo = BF16_SUBLANES
    hb = tm // halo
    convw = _group_xbc(conv_w.astype(jnp.float32))
    convb = _group_xbc(conv_b.astype(jnp.float32)).reshape(1, CONV_DIM)
    out_shape = (
        jax.ShapeDtypeStruct((bsz, s, D_INNER), act),
        jax.ShapeDtypeStruct((bsz, s, CONV_DIM), act),
        jax.ShapeDtypeStruct((bsz, s, QKV_WIDTH), act),
        jax.ShapeDtypeStruct((bsz, DILATIONS[1], s // DILATIONS[1], QKV_WIDTH), act),
        jax.ShapeDtypeStruct((bsz, DILATIONS[2], s // DILATIONS[2], QKV_WIDTH), act),
        jax.ShapeDtypeStruct((bsz, s, 2 * D_MODEL), act),
        jax.ShapeDtypeStruct((bsz, SSD_GROUPS, DT_PER_GROUP, s), jnp.float32),
    )
    tok = lambda width: pl.BlockSpec((1, tm, width), lambda b, i: (b, i, 0))
    resident = lambda shape: pl.BlockSpec(shape, lambda b, i: (0,) * len(shape),
                                          pipeline_mode=pl.Buffered(1))
    return pl.pallas_call(
        functools.partial(_in_proj_kernel, tm=tm, nc=nc),
        grid=(bsz, s // tm),
        in_specs=[
            tok(d),
            pl.BlockSpec((1, halo, d), lambda b, i: (b, jnp.maximum(i * hb - 1, 0), 0)),
            pl.BlockSpec((1, halo, d), lambda b, i: (b, jnp.minimum((i + 1) * hb, s // halo - 1), 0)),
            resident(w_packed.shape), resident(wdt_t.shape), resident(convw.shape), resident(convb.shape),
        ],
        out_specs=(
            tok(D_INNER), tok(CONV_DIM), tok(QKV_WIDTH),
            pl.BlockSpec((1, DILATIONS[1], tm // DILATIONS[1], QKV_WIDTH), lambda b, i: (b, 0, i, 0)),
            pl.BlockSpec((1, DILATIONS[2], tm // DILATIONS[2], QKV_WIDTH), lambda b, i: (b, 0, i, 0)),
            tok(2 * D_MODEL),
            pl.BlockSpec((1, SSD_GROUPS, DT_PER_GROUP, tm), lambda b, i: (b, 0, 0, i)),
        ),
        out_shape=out_shape,
        scratch_shapes=[
            pltpu.VMEM((tm + 2 * halo, d), jnp.bfloat16),
            pltpu.VMEM((2, nc // LANES, tm + 2 * halo, LANES), jnp.float32),
            pltpu.VMEM((2, nc // LANES, tm, LANES), jnp.float32),
            pltpu.VMEM((QKV_WIDTH // LANES, tm, LANES), jnp.float32),
        ],
        compiler_params=pltpu.CompilerParams(
            dimension_semantics=("arbitrary", "arbitrary"), vmem_limit_bytes=V7X_VMEM_LIMIT),
        name="in_proj",
    )(x, x, x, w_packed, wdt_t, convw, convb)


def _split3_bf16(a):
    hi = a.astype(jnp.bfloat16)
    r1 = a - hi.astype(jnp.float32)
    mid = r1.astype(jnp.bfloat16)
    lo = (r1 - mid.astype(jnp.float32)).astype(jnp.bfloat16)
    return hi, mid, lo


_ROW_CUM, _ROW_E, _ROW_W = 32, 56, 80
_STACK_PARTS = 3
_BCAST_COLS = SSD_HEADS_PER_GROUP * SSD_CHUNK + 2 * GROUP_X
LOG2_DT_FLOOR = -1e4


def _ssd_select_matrix():
    r = np.zeros((SSD_CHUNK, _BCAST_COLS), np.float32)
    hd, q = SSD_HEADS_PER_GROUP, SSD_CHUNK
    for h in range(hd):
        for part in range(_STACK_PARTS):
            r[_ROW_CUM + part * hd + h, h * q:(h + 1) * q] = 1.0
            e0 = hd * q + h * SSD_HEAD_DIM
            r[_ROW_E + part * hd + h, e0:e0 + SSD_HEAD_DIM] = 1.0
            r[_ROW_W + part * hd + h, e0 + GROUP_X:e0 + GROUP_X + SSD_HEAD_DIM] = 1.0
    return jnp.asarray(r, dtype=jnp.bfloat16)


def _ssd_blockdiag_mask():
    hd, q = SSD_HEADS_PER_GROUP, SSD_CHUNK
    m = np.zeros((_ROW_CUM, hd * q), np.float32)
    for part in range(_STACK_PARTS):
        for h in range(hd):
            m[part * hd + h, h * q:(h + 1) * q] = 1.0
    return jnp.asarray(m)


def _ssd_decay_terms(dt_raw_t, dt_bias_c, a_log_c, bdmask_ref, *, backward):
    q, hd = SSD_CHUNK, SSD_HEADS_PER_GROUP
    f32, bf16 = jnp.float32, jnp.bfloat16
    row = lax.broadcasted_iota(jnp.int32, (q, q), 0)
    col = lax.broadcasted_iota(jnp.int32, (q, q), 1)
    tri = ((row >= col) if backward else (row <= col)).astype(f32).astype(bf16)
    end = 0 if backward else q - 1
    v = dt_raw_t + dt_bias_c
    dt = jnp.maximum(v, 0.0) + jnp.log1p(jnp.exp(-jnp.abs(v)))
    a2 = dt * (-jnp.exp(a_log_c) * LOG2E)
    cum3 = jnp.dot(jnp.concatenate(_split3_bf16(a2), axis=0), tri, preferred_element_type=f32)
    cum2 = cum3[0:hd] + cum3[hd:2 * hd] + cum3[2 * hd:3 * hd]
    tot2 = cum2[:, end:end + 1]
    e_r = jnp.exp2(cum2)
    w_r = dt * jnp.exp2(tot2 - cum2)
    nrowp = jnp.maximum(jnp.log2(dt), LOG2_DT_FLOOR) - cum2
    stacked = jnp.concatenate((jnp.ones((_ROW_CUM, q), bf16),) + _split3_bf16(cum2) + _split3_bf16(e_r)
                              + _split3_bf16(w_r) + (jnp.zeros((q - _ROW_W - _STACK_PARTS * hd, q), bf16),),
                              axis=0)
    cols = stacked.astype(f32).T.astype(bf16)
    nparts = jnp.concatenate(_split3_bf16(nrowp) + (jnp.zeros((hd, q), bf16),), axis=0).astype(f32)
    data_rows = (jnp.tile(nparts, (1, hd)) * bdmask_ref[...]).astype(bf16)
    return cols, data_rows


def _ssd_intra(xq, bq, cq, big_diff, w_exp, *, backward):
    q, hd = SSD_CHUNK, SSD_HEADS_PER_GROUP
    f32, bf16 = jnp.float32, jnp.bfloat16
    row = lax.broadcasted_iota(jnp.int32, (q, q), 0)
    col = lax.broadcasted_iota(jnp.int32, (q, q), 1)
    mask = (row <= col) if backward else (row >= col)
    cb = lax.dot_general(cq, bq, (((1,), (1,)), ((), ())), preferred_element_type=f32)
    lane = lax.broadcasted_iota(jnp.int32, (q, LANES), 1)
    first_half = lane < SSD_HEAD_DIM
    ys = []
    for j in range(hd // 2):
        xpair = xq[:, j * LANES:(j + 1) * LANES]
        ms = []
        for h in (2 * j, 2 * j + 1):
            diff = big_diff[:, h * q:(h + 1) * q]
            ms.append((cb * jnp.exp2(jnp.where(mask, diff, NEG_BIG))).astype(bf16))
        lhs = jnp.concatenate(ms, axis=1)
        zero = jnp.zeros_like(xpair)
        rhs = jnp.concatenate([jnp.where(first_half, xpair, zero),
                               jnp.where(first_half, zero, xpair)], axis=0)
        ys.append(jnp.dot(lhs, rhs, preferred_element_type=f32))
    xw = xq * w_exp.astype(bf16)
    s_new = lax.dot_general(bq, xw, (((0,), (0,)), ((), ())), preferred_element_type=f32)
    return jnp.concatenate(ys, axis=1), s_new


def _ssd_kernel(xc_ref, dtt_ref, z_ref, dtb_ref, alog_ref, dskip_ref, normw_ref, sel_ref, bdmask_ref, y_ref,
                yf_ref, h_ref, ew_ref, *, lb, nb):
    ph = pl.program_id(2)
    i = pl.program_id(3)
    q = SSD_CHUNK
    hd = SSD_HEADS_PER_GROUP
    nck = lb // q
    f32, bf16 = jnp.float32, jnp.bfloat16
    blk = lambda c: slice(c * q, (c + 1) * q)

    def scan_block(*, backward, finish):
        hs = slice(hd, DT_PER_GROUP) if backward else slice(0, hd)
        order = list(reversed(range(nck))) if backward else list(range(nck))
        end = 0 if backward else q - 1
        terms = [_ssd_decay_terms(dtt_ref[0, 0, hs, blk(c)], dtb_ref[0, hs, :], alog_ref[0, hs, :],
                                  bdmask_ref, backward=backward) for c in range(nck)]

        def broadcasts(c):
            cols, data_rows = terms[c]
            ew_ref[blk(c), :] = jnp.dot(cols, sel_ref[:, hd * q:], preferred_element_type=f32)
            r_diff = jnp.concatenate([data_rows, sel_ref[_ROW_CUM:, 0:hd * q]], axis=0)
            return jnp.dot(cols, r_diff, preferred_element_type=f32)

        h = h_ref[...]
        big_diff = broadcasts(order[0])
        for n, c in enumerate(order):
            next_diff = broadcasts(order[n + 1]) if n + 1 < nck else None
            cq = xc_ref[0, blk(c), GROUP_X + D_STATE:GROUP_XBC]
            yd, sn = _ssd_intra(xc_ref[0, blk(c), 0:GROUP_X], xc_ref[0, blk(c), GROUP_X:GROUP_X + D_STATE], cq,
                                big_diff, ew_ref[blk(c), GROUP_X:], backward=backward)
            ch = jnp.dot(cq, h.astype(bf16), preferred_element_type=f32)
            finish(c, yd + ew_ref[blk(c), 0:GROUP_X] * ch)
            decay = ew_ref[c * q + end:c * q + end + 1, 0:GROUP_X]
            h = decay * h + sn
            big_diff = next_diff
        h_ref[...] = h

    @pl.when(i == 0)
    def _():
        h_ref[...] = jnp.zeros_like(h_ref)

    @pl.when(ph == 0)
    def _forward():
        base = pl.multiple_of(i * lb, lb)

        def finish(c, y):
            yf_ref[pl.ds(base + c * q, q), :] = y

        scan_block(backward=False, finish=finish)

    @pl.when(ph == 1)
    def _backward():
        base = pl.multiple_of((nb - 1 - i) * lb, lb)

        def finish(c, y):
            y = y + yf_ref[pl.ds(base + c * q, q), :] + dskip_ref[0] * xc_ref[0, blk(c), 0:GROUP_X].astype(f32)
            zq = z_ref[0, blk(c), :].astype(f32)
            y = y * (zq * _sigmoid(zq))
            ms = jnp.mean(y * y, axis=-1, keepdims=True)
            y_ref[0, blk(c), :] = (y * lax.rsqrt(ms + NORM_EPS) * normw_ref[0]).astype(y_ref.dtype)

        scan_block(backward=True, finish=finish)


def _ssd(xc, dtt, z, dt_bias_f, dt_bias_b, a_log_f, a_log_b, d_skip, ssd_norm_w, *, lb=2048):
    bsz, s, _ = xc.shape
    nb = s // lb
    assert s % lb == 0 and lb % SSD_CHUNK == 0
    f32 = jnp.float32
    per_group = lambda f, b: jnp.concatenate(
        [f.astype(f32).reshape(SSD_GROUPS, SSD_HEADS_PER_GROUP), b.astype(f32).reshape(SSD_GROUPS, SSD_HEADS_PER_GROUP)],
        axis=1).reshape(SSD_GROUPS, DT_PER_GROUP, 1)
    dtb = per_group(dt_bias_f, dt_bias_b)
    alog = per_group(a_log_f, a_log_b)
    dskip = jnp.repeat(d_skip.astype(f32), SSD_HEAD_DIM).reshape(SSD_GROUPS, 1, GROUP_X)
    normw = ssd_norm_w.astype(f32).reshape(SSD_GROUPS, 1, GROUP_X)
    sel, bdmask = _ssd_select_matrix(), _ssd_blockdiag_mask()

    any_blk = lambda ph, i: jnp.where(ph == 0, i, nb - 1 - i)
    bwd_blk = lambda ph, i: jnp.where(ph == 0, nb - 1, nb - 1 - i)
    par = lambda shape: pl.BlockSpec(shape, lambda b, g, ph, i: (g, 0, 0))
    return pl.pallas_call(
        functools.partial(_ssd_kernel, lb=lb, nb=nb),
        grid=(bsz, SSD_GROUPS, 2, nb),
        in_specs=[
            pl.BlockSpec((1, lb, GROUP_XBC), lambda b, g, ph, i: (b, any_blk(ph, i), g)),
            pl.BlockSpec((1, 1, DT_PER_GROUP, lb), lambda b, g, ph, i: (b, g, 0, any_blk(ph, i))),
            pl.BlockSpec((1, lb, GROUP_X), lambda b, g, ph, i: (b, bwd_blk(ph, i), g)),
            par((1, DT_PER_GROUP, 1)), par((1, DT_PER_GROUP, 1)),
            par((1, 1, GROUP_X)), par((1, 1, GROUP_X)),
            pl.BlockSpec(sel.shape, lambda b, g, ph, i: (0, 0)),
            pl.BlockSpec(bdmask.shape, lambda b, g, ph, i: (0, 0)),
        ],
        out_specs=pl.BlockSpec((1, lb, GROUP_X), lambda b, g, ph, i: (b, bwd_blk(ph, i), g)),
        out_shape=jax.ShapeDtypeStruct((bsz, s, D_INNER), jnp.bfloat16),
        scratch_shapes=[
            pltpu.VMEM((s, GROUP_X), f32),
            pltpu.VMEM((D_STATE, GROUP_X), f32),
            pltpu.VMEM((lb, 2 * GROUP_X), f32),
        ],
        compiler_params=pltpu.CompilerParams(
            dimension_semantics=("arbitrary",) * 4, vmem_limit_bytes=V7X_VMEM_LIMIT),
        name="ssd",
    )(xc, dtt, z, dtb, alog, dskip, normw, sel, bdmask)


ATTN_QB = 128
ATTN_WIN = ATTN_QB + 2 * ATTN_HALF
ATTN_BLOCKS_PER_ITER = 2


def _attn_bias_table():
    heads = len(DILATIONS) * HEADS_PER_PATTERN
    slopes = 2.0 ** (-8.0 * np.arange(1, heads + 1) / heads)
    rel = (np.arange(ATTN_WIN)[None, :] - ATTN_HALF) - np.arange(ATTN_QB)[:, None]
    tabs = []
    for h in range(heads):
        dil = DILATIONS[h // HEADS_PER_PATTERN]
        tabs.append(np.where(np.abs(rel) <= ATTN_HALF, -slopes[h] * np.abs(rel) * dil, NEG_BIG))
    return jnp.asarray(np.stack(tabs), dtype=jnp.float32)


def _attn_kernel(bias_ref, q1_ref, p1_ref, n1_ref, q2_ref, p2_ref, n2_ref, q3_ref, p3_ref, n3_ref, y_ref,
                 kv1_ref, kv2_ref, kv3_ref, o_ref, lse_ref, *, ta, nt, seq):
    i = pl.program_id(1)
    f32, bf16 = jnp.float32, jnp.bfloat16
    lane = lax.broadcasted_iota(jnp.int32, (ATTN_QB, LANES), 1)
    first_half = lane < ATTN_HEAD_DIM
    kcol = lax.broadcasted_iota(jnp.int32, (1, ATTN_WIN), 1)

    for pat, (dil, q_ref, prev_ref, next_ref, kv_ref) in enumerate((
            (DILATIONS[0], q1_ref, p1_ref, n1_ref, kv1_ref),
            (DILATIONS[1], q2_ref, p2_ref, n2_ref, kv2_ref),
            (DILATIONS[2], q3_ref, p3_ref, n3_ref, kv3_ref))):
        n = ta // dil
        seq_l = seq // dil
        kv_ref[:, 0:ATTN_HALF, :] = prev_ref[0, :, :, PATTERN_WIDTH:]
        kv_ref[:, ATTN_HALF:ATTN_HALF + n, :] = q_ref[0, :, :, PATTERN_WIDTH:]
        kv_ref[:, ATTN_HALF + n:2 * ATTN_HALF + n, :] = next_ref[0, :, :, PATTERN_WIDTH:]
        nqb = n // ATTN_QB

        def block(it, carry, dil=dil, q_ref=q_ref, kv_ref=kv_ref, nqb=nqb, n=n, seq_l=seq_l, pat=pat):
            nh = HEADS_PER_PATTERN
            items = range(ATTN_BLOCKS_PER_ITER * nh)
            places, pens, vps, scs = [], [], [], []
            for k in range(ATTN_BLOCKS_PER_ITER):
                idx = it * ATTN_BLOCKS_PER_ITER + k
                r = idx // nqb
                row0 = pl.multiple_of((idx % nqb) * ATTN_QB, ATTN_QB)
                places.append((r, row0))
                kpos = i * n + row0 - ATTN_HALF + kcol
                pens.append(jnp.where((kpos >= 0) & (kpos < seq_l), 0.0, NEG_BIG).astype(f32))
                for pair in range(nh // 2):
                    cs = slice(pair * LANES, (pair + 1) * LANES)
                    qp = q_ref[0, r, pl.ds(row0, ATTN_QB), cs]
                    kp = kv_ref[r, pl.ds(row0, ATTN_WIN), cs]
                    vps.append(kv_ref[r, pl.ds(row0, ATTN_WIN),
                                      pair * LANES + PATTERN_WIDTH:(pair + 1) * LANES + PATTERN_WIDTH])
                    zero = jnp.zeros_like(qp)
                    for qh in (jnp.where(first_half, qp, zero), jnp.where(first_half, zero, qp)):
                        scs.append(lax.dot_general(qh, kp, (((1,), (1,)), ((), ())), preferred_element_type=f32))
            scs = [scs[t] + bias_ref[pat * nh + t % nh] + pens[t // nh] for t in items]
            ms = [jnp.max(scs[t], axis=-1, keepdims=True) for t in items]
            pes = [jnp.exp(scs[t] - ms[t]) for t in items]
            dens = [jnp.sum(pes[t], axis=-1, keepdims=True) for t in items]
            pvs = [jnp.dot(pes[t].astype(bf16), vps[t // 2], preferred_element_type=f32) for t in items]
            outs = [pvs[t] / dens[t] for t in items]
            lses = [ms[t] + jnp.log(dens[t]) for t in items]
            for k, (r, row0) in enumerate(places):
                rows = pl.ds(r + row0 * dil, ATTN_QB, stride=dil)
                for pair in range(nh // 2):
                    t = k * nh + 2 * pair
                    o_ref[pat, pair, rows, :] = jnp.where(first_half, outs[t], outs[t + 1])
                    lse_ref[pat, pair, rows, :] = jnp.where(first_half, lses[t], lses[t + 1])
            return carry

        lax.fori_loop(0, dil * nqb // ATTN_BLOCKS_PER_ITER, block, 0)

    for pair in range(HEADS_PER_PATTERN // 2):
        l0, l1, l2 = lse_ref[0, pair], lse_ref[1, pair], lse_ref[2, pair]
        m = jnp.maximum(jnp.maximum(l0, l1), l2)
        w0, w1, w2 = jnp.exp(l0 - m), jnp.exp(l1 - m), jnp.exp(l2 - m)
        y = (w0 * o_ref[0, pair] + w1 * o_ref[1, pair] + w2 * o_ref[2, pair]) / (w0 + w1 + w2)
        y_ref[0, :, pair * LANES:(pair + 1) * LANES] = y.astype(y_ref.dtype)


def _attention(qkv1, qkv2, qkv3, *, ta=2048):
    bsz, s, _ = qkv1.shape
    nt = s // ta
    assert s % ta == 0 and ta % (ATTN_QB * DILATIONS[2]) == 0 and (ta // ATTN_QB) % ATTN_BLOCKS_PER_ITER == 0
    qkv1 = qkv1.reshape(bsz, 1, s, QKV_WIDTH)
    hb = ATTN_HALF
    specs = []
    for dil in DILATIONS:
        n = ta // dil
        last = s // dil // hb - 1
        specs += [
            pl.BlockSpec((1, dil, n, QKV_WIDTH), lambda b, i: (b, 0, i, 0)),
            pl.BlockSpec((1, dil, hb, QKV_WIDTH), lambda b, i, n=n: (b, 0, jnp.maximum(i * (n // hb) - 1, 0), 0)),
            pl.BlockSpec((1, dil, hb, QKV_WIDTH),
                         lambda b, i, n=n, last=last: (b, 0, jnp.minimum((i + 1) * (n // hb), last), 0)),
        ]
    bias = _attn_bias_table()
    kvw = 2 * PATTERN_WIDTH
    return pl.pallas_call(
        functools.partial(_attn_kernel, ta=ta, nt=nt, seq=s),
        grid=(bsz, nt),
        in_specs=[pl.BlockSpec(bias.shape, lambda b, i: (0, 0, 0))] + specs,
        out_specs=pl.BlockSpec((1, ta, PATTERN_WIDTH), lambda b, i: (b, i, 0)),
        out_shape=jax.ShapeDtypeStruct((bsz, s, PATTERN_WIDTH), jnp.bfloat16),
        scratch_shapes=[pltpu.VMEM((dil, ta // dil + 2 * hb, kvw), jnp.bfloat16) for dil in DILATIONS] + [
            pltpu.VMEM((len(DILATIONS), PATTERN_WIDTH // LANES, ta, LANES), jnp.float32),
            pltpu.VMEM((len(DILATIONS), PATTERN_WIDTH // LANES, ta, LANES), jnp.float32),
        ],
        compiler_params=pltpu.CompilerParams(
            dimension_semantics=("arbitrary", "arbitrary"), vmem_limit_bytes=V7X_VMEM_LIMIT),
        name="dilated_attn",
    )(bias, qkv1, qkv1, qkv1, qkv2, qkv2, qkv2, qkv3, qkv3, qkv3)


def _layer_norm(v, g, b):
    mu = jnp.mean(v, axis=-1, keepdims=True)
    c = v - mu
    var = jnp.mean(c * c, axis=-1, keepdims=True)
    return c * lax.rsqrt(var + NORM_EPS) * g + b


def _merge_mlp_kernel(x_ref, yssd_ref, yatt_ref, gate_ref, bgate_ref, wps_ref, wpa_ref, wout_ref,
                      ln1g_ref, ln1b_ref, wup_ref, wdown_ref, ln2g_ref, ln2b_ref, o_ref, *, ffc):
    f32, bf16 = jnp.float32, jnp.bfloat16
    ys = jnp.dot(yssd_ref[0], wps_ref[...], preferred_element_type=f32)
    ya = jnp.dot(yatt_ref[0], wpa_ref[...], preferred_element_type=f32)
    g_ssd = _sigmoid(gate_ref[0, :, 0:D_MODEL].astype(f32) + bgate_ref[:, 0:D_MODEL])
    g_att = _sigmoid(gate_ref[0, :, D_MODEL:2 * D_MODEL].astype(f32) + bgate_ref[:, D_MODEL:2 * D_MODEL])
    mix = jnp.dot((g_ssd * ys + g_att * ya).astype(bf16), wout_ref[...], preferred_element_type=f32)
    h = _layer_norm(ALPHA * x_ref[0] + mix, ln1g_ref[...], ln1b_ref[...])
    hb = h.astype(bf16)
    f = jnp.zeros_like(h)
    for c in range(0, D_FF, ffc):
        up = jnp.maximum(jnp.dot(hb, wup_ref[:, c:c + ffc], preferred_element_type=f32), 0.0)
        f = f + jnp.dot((up * up).astype(bf16), wdown_ref[c:c + ffc, :], preferred_element_type=f32)
    o_ref[0] = _layer_norm(ALPHA * h + f, ln2g_ref[...], ln2b_ref[...]).astype(o_ref.dtype)


def _merge_mlp(x, y_ssd, y_att, gate, b_gate, w_proj_ssd, w_proj_attn, w_out, ln1_g, ln1_b, w_up, w_down,
               ln2_g, ln2_b, *, tm=512, ffc=1024):
    bsz, s, d = x.shape
    assert s % tm == 0 and D_FF % ffc == 0
    bf16, f32 = jnp.bfloat16, jnp.float32
    row = lambda v: v.astype(f32).reshape(1, -1)
    tok = lambda width: pl.BlockSpec((1, tm, width), lambda b, i: (b, i, 0))
    resident = lambda shape: pl.BlockSpec(shape, lambda b, i: (0,) * len(shape), pipeline_mode=pl.Buffered(1))
    operands = [
        (x, tok(d)), (y_ssd, tok(D_INNER)), (y_att, tok(PATTERN_WIDTH)), (gate, tok(2 * D_MODEL)),
        (row(b_gate), None), (w_proj_ssd.astype(bf16), None), (w_proj_attn.astype(bf16), None),
        (w_out.astype(bf16), None), (row(ln1_g), None), (row(ln1_b), None),
        (w_up.astype(bf16), None), (w_down.astype(bf16), None), (row(ln2_g), None), (row(ln2_b), None),
    ]
    args = [a for a, _ in operands]
    in_specs = [spec if spec is not None else resident(a.shape) for a, spec in operands]
    return pl.pallas_call(
        functools.partial(_merge_mlp_kernel, ffc=ffc),
        grid=(bsz, s // tm),
        in_specs=in_specs,
        out_specs=tok(d),
        out_shape=jax.ShapeDtypeStruct((bsz, s, d), x.dtype),
        compiler_params=pltpu.CompilerParams(
            dimension_semantics=("arbitrary", "arbitrary"), vmem_limit_bytes=V7X_VMEM_LIMIT),
        name="merge_mlp",
    )(*args)


def kernel(x, w_in, b_gate, conv_w, conv_b, dt_bias_f, dt_bias_b, a_log_f, a_log_b, d_skip, ssd_norm_w,
           w_proj_ssd, w_proj_attn, w_out, ln1_g, ln1_b, w_up, w_down, ln2_g, ln2_b):
    w_packed, wdt_t = _pack_w_in(w_in)
    z, xc, qkv1, qkv2, qkv3, gate, dtt = _in_proj(x, w_packed, wdt_t, conv_w, conv_b)
    y_ssd = _ssd(xc, dtt, z, dt_bias_f, dt_bias_b, a_log_f, a_log_b, d_skip, ssd_norm_w)
    y_att = _attention(qkv1, qkv2, qkv3)
    return _merge_mlp(x, y_ssd, y_att, gate, b_gate, w_proj_ssd, w_proj_attn, w_out, ln1_g, ln1_b,
                      w_up, w_down, ln2_g, ln2_b)
```

```python
import functools
import math

import jax
import jax.numpy as jnp
import numpy as np
from jax import lax
from jax.experimental import pallas as pl
from jax.experimental.pallas import tpu as pltpu

D_MODEL = 1024
D_INNER = 2048
SSD_HEAD_DIM = 64
SSD_GROUPS = 4
SSD_HEADS_PER_GROUP = 8
D_STATE = 128
D_CONV = 5
SSD_CHUNK = 128
GROUP_X = SSD_HEADS_PER_GROUP * SSD_HEAD_DIM
GROUP_XBC = GROUP_X + 2 * D_STATE
CONV_DIM = D_INNER + 2 * SSD_GROUPS * D_STATE
NORM_EPS = 1e-5
ATTN_HEAD_DIM = 64
DILATIONS = (1, 4, 16)
ATTN_HALF = 64
HEADS_PER_PATTERN = 4
PATTERN_WIDTH = HEADS_PER_PATTERN * ATTN_HEAD_DIM
QKV_WIDTH = 3 * PATTERN_WIDTH
ATTN_WIDTH = 3 * PATTERN_WIDTH
D_FF = 4 * D_MODEL
DT_PER_GROUP = 2 * SSD_HEADS_PER_GROUP
DT_ROWS = SSD_GROUPS * DT_PER_GROUP
ALPHA = 2.0 ** 0.25
NEG_BIG = -1e30
LOG2E = 1.4426950408889634

V7X_VMEM_LIMIT = 56 * 1024 * 1024
LANES = 128
SUBLANES = 8
BF16_SUBLANES = 16
CONV_ROW_STRIDE = 4

_OFF_Z = 0
_OFF_XBC = _OFF_Z + D_INNER
_OFF_QKV = _OFF_XBC + CONV_DIM
_OFF_GATE = _OFF_QKV + 3 * QKV_WIDTH
IN_COLS_PACKED = _OFF_GATE + 2 * D_MODEL


def _group_xbc(a):
    nb = SSD_GROUPS * D_STATE
    xs, bs, cs = a[..., :D_INNER], a[..., D_INNER:D_INNER + nb], a[..., D_INNER + nb:]
    parts = []
    for g in range(SSD_GROUPS):
        parts += [xs[..., g * GROUP_X:(g + 1) * GROUP_X], bs[..., g * D_STATE:(g + 1) * D_STATE],
                  cs[..., g * D_STATE:(g + 1) * D_STATE]]
    return jnp.concatenate(parts, axis=-1)


def _pack_w_in(w_in):
    splits = np.cumsum([D_INNER, CONV_DIM, 32, 32, ATTN_WIDTH, ATTN_WIDTH, ATTN_WIDTH])
    wz, wxbc, wdtf, wdtb, wq, wk, wv, wg = jnp.split(w_in, list(splits), axis=1)
    wxbc = _group_xbc(wxbc)
    wq = wq * (1.0 / math.sqrt(ATTN_HEAD_DIM))
    qkv = []
    for p in range(3):
        s = slice(p * PATTERN_WIDTH, (p + 1) * PATTERN_WIDTH)
        qkv += [wq[:, s], wk[:, s], wv[:, s]]
    dts = []
    for g in range(SSD_GROUPS):
        s = slice(g * SSD_HEADS_PER_GROUP, (g + 1) * SSD_HEADS_PER_GROUP)
        dts += [wdtf[:, s], wdtb[:, s]]
    packed = jnp.concatenate([wz, wxbc] + qkv + [wg], axis=1).astype(jnp.bfloat16)
    wdt_t = jnp.concatenate(dts, axis=1).T.astype(jnp.bfloat16)
    return packed, wdt_t


def _sigmoid(v):
    return 0.5 * (1.0 + jnp.tanh(0.5 * v))


def _in_proj_kernel(x_ref, xprev_ref, xnext_ref, w_ref, wdt_t_ref, convw_ref, convb_ref,
                    z_ref, xbc_ref, qkv1_ref, qkv2_ref, qkv3_ref, gate_ref, dtt_ref,
                    xb_ref, cin_ref, cout_ref, stage_ref, *, tm, nc):
    i = pl.program_id(1)
    f32, bf16 = jnp.float32, jnp.bfloat16
    halo = BF16_SUBLANES
    pad = D_CONV // 2
    xb_ref[0:halo, :] = jnp.where(i > 0, xprev_ref[0], 0.0).astype(bf16)
    xb_ref[halo:halo + tm, :] = x_ref[0].astype(bf16)
    xb_ref[halo + tm:2 * halo + tm, :] = jnp.where(i < pl.num_programs(1) - 1, xnext_ref[0], 0.0).astype(bf16)

    def proj(off, width):
        return jnp.dot(xb_ref[halo:halo + tm, :], w_ref[:, off:off + width], preferred_element_type=f32)

    nslab = nc // LANES

    def store(out_ref, off, c):
        def run():
            out_ref[0, :, c:c + nc] = proj(off + c, nc).astype(out_ref.dtype)
        return run

    def qkv_dilated(out_ref, pat, dil, u):
        def run():
            res = proj(_OFF_QKV + pat * QKV_WIDTH + u * nc, nc)
            for c in range(nslab):
                stage_ref[u * nslab + c] = res[:, c * LANES:(c + 1) * LANES]
            for r in range(dil):
                for c in range(nslab):
                    out_ref[0, r, :, u * nc + c * LANES:u * nc + (c + 1) * LANES] = (
                        stage_ref[u * nslab + c, pl.ds(r, tm // dil, stride=dil), :].astype(out_ref.dtype))
        return run

    def dt_rows():
        dtt = lax.dot_general(wdt_t_ref[...], xb_ref[halo:halo + tm, :], (((1,), (1,)), ((), ())),
                              preferred_element_type=f32)
        for g in range(SSD_GROUPS):
            dtt_ref[0, g] = dtt[g * DT_PER_GROUP:(g + 1) * DT_PER_GROUP, :]

    def conv_slab(buf, s, cs):
        taps = [jnp.broadcast_to(convw_ref[k:k + 1, cs], (SUBLANES, LANES)) for k in range(D_CONV)]
        bias = jnp.broadcast_to(convb_ref[:, cs], (SUBLANES, LANES))
        for r0 in range(0, tm, SUBLANES * CONV_ROW_STRIDE):
            for v in range(CONV_ROW_STRIDE):
                acc = bias
                for k in range(D_CONV):
                    lo = halo - pad + k + r0 + v
                    acc = acc + taps[k] * cin_ref[buf, s, pl.ds(lo, SUBLANES, stride=CONV_ROW_STRIDE), :]
                cout_ref[buf, s, pl.ds(r0 + v, SUBLANES, stride=CONV_ROW_STRIDE), :] = acc * _sigmoid(acc)
        xbc_ref[0, :, cs] = cout_ref[buf, s].astype(xbc_ref.dtype)

    others = ([store(z_ref, _OFF_Z, c) for c in range(0, D_INNER, nc)]
              + [store(gate_ref, _OFF_GATE, c) for c in range(0, 2 * D_MODEL, nc)]
              + [store(qkv1_ref, _OFF_QKV, c) for c in range(0, QKV_WIDTH, nc)]
              + [qkv_dilated(qkv2_ref, 1, DILATIONS[1], u) for u in range(QKV_WIDTH // nc)]
              + [qkv_dilated(qkv3_ref, 2, DILATIONS[2], u) for u in range(QKV_WIDTH // nc)] + [dt_rows])
    nchunk = CONV_DIM // nc
    for c in range(nchunk):
        buf = c % 2
        ext = jnp.dot(xb_ref[...], w_ref[:, _OFF_XBC + c * nc:_OFF_XBC + (c + 1) * nc],
                      preferred_element_type=f32)
        for s in range(nslab):
            cin_ref[buf, s] = ext[:, s * LANES:(s + 1) * LANES]
        for s in range(nslab):
            if others:
                others.pop(0)()
            conv_slab(buf, s, slice(c * nc + s * LANES, c * nc + (s + 1) * LANES))
    for job in others:
        job()


def _in_proj(x, w_packed, wdt_t, conv_w, conv_b, *, tm=512, nc=256):
    bsz, s, d = x.shape
    assert s % tm == 0 and tm % (BF16_SUBLANES * DILATIONS[2]) == 0 and CONV_DIM % nc == 0
    act = jnp.bfloat16
    halo = BF16_SUBLANES
    hb = tm // halo
    convw = _group_xbc(conv_w.astype(jnp.float32))
    convb = _group_xbc(conv_b.astype(jnp.float32)).reshape(1, CONV_DIM)
    out_shape = (
        jax.ShapeDtypeStruct((bsz, s, D_INNER), act),
        jax.ShapeDtypeStruct((bsz, s, CONV_DIM), act),
        jax.ShapeDtypeStruct((bsz, s, QKV_WIDTH), act),
        jax.ShapeDtypeStruct((bsz, DILATIONS[1], s // DILATIONS[1], QKV_WIDTH), act),
        jax.ShapeDtypeStruct((bsz, DILATIONS[2], s // DILATIONS[2], QKV_WIDTH), act),
        jax.ShapeDtypeStruct((bsz, s, 2 * D_MODEL), act),
        jax.ShapeDtypeStruct((bsz, SSD_GROUPS, DT_PER_GROUP, s), jnp.float32),
    )
    tok = lambda width: pl.BlockSpec((1, tm, width), lambda b, i: (b, i, 0))
    resident = lambda shape: pl.BlockSpec(shape, lambda b, i: (0,) * len(shape),
                                          pipeline_mode=pl.Buffered(1))
    return pl.pallas_call(
        functools.partial(_in_proj_kernel, tm=tm, nc=nc),
        grid=(bsz, s // tm),
        in_specs=[
            tok(d),
            pl.BlockSpec((1, halo, d), lambda b, i: (b, jnp.maximum(i * hb - 1, 0), 0)),
            pl.BlockSpec((1, halo, d), lambda b, i: (b, jnp.minimum((i + 1) * hb, s // halo - 1), 0)),
            resident(w_packed.shape), resident(wdt_t.shape), resident(convw.shape), resident(convb.shape),
        ],
        out_specs=(
            tok(D_INNER), tok(CONV_DIM), tok(QKV_WIDTH),
            pl.BlockSpec((1, DILATIONS[1], tm // DILATIONS[1], QKV_WIDTH), lambda b, i: (b, 0, i, 0)),
            pl.BlockSpec((1, DILATIONS[2], tm // DILATIONS[2], QKV_WIDTH), lambda b, i: (b, 0, i, 0)),
            tok(2 * D_MODEL),
            pl.BlockSpec((1, SSD_GROUPS, DT_PER_GROUP, tm), lambda b, i: (b, 0, 0, i)),
        ),
        out_shape=out_shape,
        scratch_shapes=[
            pltpu.VMEM((tm + 2 * halo, d), jnp.bfloat16),
            pltpu.VMEM((2, nc // LANES, tm + 2 * halo, LANES), jnp.float32),
            pltpu.VMEM((2, nc // LANES, tm, LANES), jnp.float32),
            pltpu.VMEM((QKV_WIDTH // LANES, tm, LANES), jnp.float32),
        ],
        compiler_params=pltpu.CompilerParams(
            dimension_semantics=("arbitrary", "arbitrary"), vmem_limit_bytes=V7X_VMEM_LIMIT),
        name="in_proj",
    )(x, x, x, w_packed, wdt_t, convw, convb)


def _split3_bf16(a):
    hi = a.astype(jnp.bfloat16)
    r1 = a - hi.astype(jnp.float32)
    mid = r1.astype(jnp.bfloat16)
    lo = (r1 - mid.astype(jnp.float32)).astype(jnp.bfloat16)
    return hi, mid, lo


_ROW_CUM, _ROW_E, _ROW_W = 32, 56, 80
_STACK_PARTS = 3
_BCAST_COLS = SSD_HEADS_PER_GROUP * SSD_CHUNK + 2 * GROUP_X
LOG2_DT_FLOOR = -1e4


def _ssd_select_matrix():
    r = np.zeros((SSD_CHUNK, _BCAST_COLS), np.float32)
    hd, q = SSD_HEADS_PER_GROUP, SSD_CHUNK
    for h in range(hd):
        for part in range(_STACK_PARTS):
            r[_ROW_CUM + part * hd + h, h * q:(h + 1) * q] = 1.0
            e0 = hd * q + h * SSD_HEAD_DIM
            r[_ROW_E + part * hd + h, e0:e0 + SSD_HEAD_DIM] = 1.0
            r[_ROW_W + part * hd + h, e0 + GROUP_X:e0 + GROUP_X + SSD_HEAD_DIM] = 1.0
    return jnp.asarray(r, dtype=jnp.bfloat16)


def _ssd_blockdiag_mask():
    hd, q = SSD_HEADS_PER_GROUP, SSD_CHUNK
    m = np.zeros((_ROW_CUM, hd * q), np.float32)
    for part in range(_STACK_PARTS):
        for h in range(hd):
            m[part * hd + h, h * q:(h + 1) * q] = 1.0
    return jnp.asarray(m)


def _ssd_decay_terms(dt_raw_t, dt_bias_c, a_log_c, bdmask_ref, *, backward):
    q, hd = SSD_CHUNK, SSD_HEADS_PER_GROUP
    f32, bf16 = jnp.float32, jnp.bfloat16
    row = lax.broadcasted_iota(jnp.int32, (q, q), 0)
    col = lax.broadcasted_iota(jnp.int32, (q, q), 1)
    tri = ((row >= col) if backward else (row <= col)).astype(f32).astype(bf16)
    end = 0 if backward else q - 1
    v = dt_raw_t + dt_bias_c
    dt = jnp.maximum(v, 0.0) + jnp.log1p(jnp.exp(-jnp.abs(v)))
    a2 = dt * (-jnp.exp(a_log_c) * LOG2E)
    cum3 = jnp.dot(jnp.concatenate(_split3_bf16(a2), axis=0), tri, preferred_element_type=f32)
    cum2 = cum3[0:hd] + cum3[hd:2 * hd] + cum3[2 * hd:3 * hd]
    tot2 = cum2[:, end:end + 1]
    e_r = jnp.exp2(cum2)
    w_r = dt * jnp.exp2(tot2 - cum2)
    nrowp = jnp.maximum(jnp.log2(dt), LOG2_DT_FLOOR) - cum2
    stacked = jnp.concatenate((jnp.ones((_ROW_CUM, q), bf16),) + _split3_bf16(cum2) + _split3_bf16(e_r)
                              + _split3_bf16(w_r) + (jnp.zeros((q - _ROW_W - _STACK_PARTS * hd, q), bf16),),
                              axis=0)
    cols = stacked.astype(f32).T.astype(bf16)
    nparts = jnp.concatenate(_split3_bf16(nrowp) + (jnp.zeros((hd, q), bf16),), axis=0).astype(f32)
    data_rows = (jnp.tile(nparts, (1, hd)) * bdmask_ref[...]).astype(bf16)
    return cols, data_rows


def _ssd_decay_matrices(bq, cq, big_diff, *, backward):
    q, hd = SSD_CHUNK, SSD_HEADS_PER_GROUP
    f32, bf16 = jnp.float32, jnp.bfloat16
    row = lax.broadcasted_iota(jnp.int32, (q, q), 0)
    col = lax.broadcasted_iota(jnp.int32, (q, q), 1)
    mask = (row <= col) if backward else (row >= col)
    cb = lax.dot_general(cq, bq, (((1,), (1,)), ((), ())), preferred_element_type=f32)
    pairs = []
    for j in range(hd // 2):
        ms = [(cb * jnp.exp2(jnp.where(mask, big_diff[:, h * q:(h + 1) * q], NEG_BIG))).astype(bf16)
              for h in (2 * j, 2 * j + 1)]
        pairs.append(jnp.concatenate(ms, axis=1))
    return pairs


def _ssd_products(xq, bq, pairs, w_exp):
    q = SSD_CHUNK
    f32, bf16 = jnp.float32, jnp.bfloat16
    lane = lax.broadcasted_iota(jnp.int32, (q, LANES), 1)
    first_half = lane < SSD_HEAD_DIM
    ys = []
    for j, lhs in enumerate(pairs):
        xpair = xq[:, j * LANES:(j + 1) * LANES]
        zero = jnp.zeros_like(xpair)
        rhs = jnp.concatenate([jnp.where(first_half, xpair, zero),
                               jnp.where(first_half, zero, xpair)], axis=0)
        ys.append(jnp.dot(lhs, rhs, preferred_element_type=f32))
    xw = xq * w_exp.astype(bf16)
    s_new = lax.dot_general(bq, xw, (((0,), (0,)), ((), ())), preferred_element_type=f32)
    return jnp.concatenate(ys, axis=1), s_new


def _ssd_kernel(xc_ref, dtt_ref, z_ref, dtb_ref, alog_ref, dskip_ref, normw_ref, sel_ref, bdmask_ref, y_ref,
                yf_ref, h_ref, ew_ref, *, lb, nb):
    ph = pl.program_id(2)
    i = pl.program_id(3)
    q = SSD_CHUNK
    hd = SSD_HEADS_PER_GROUP
    nck = lb // q
    f32, bf16 = jnp.float32, jnp.bfloat16
    blk = lambda c: slice(c * q, (c + 1) * q)

    def scan_block(*, backward, finish):
        hs = slice(hd, DT_PER_GROUP) if backward else slice(0, hd)
        order = list(reversed(range(nck))) if backward else list(range(nck))
        end = 0 if backward else q - 1
        terms = [_ssd_decay_terms(dtt_ref[0, 0, hs, blk(c)], dtb_ref[0, hs, :], alog_ref[0, hs, :],
                                  bdmask_ref, backward=backward) for c in range(nck)]

        def broadcasts(c):
            cols, data_rows = terms[c]
            ew_ref[blk(c), :] = jnp.dot(cols, sel_ref[:, hd * q:], preferred_element_type=f32)
            r_diff = jnp.concatenate([data_rows, sel_ref[_ROW_CUM:, 0:hd * q]], axis=0)
            return jnp.dot(cols, r_diff, preferred_element_type=f32)

        b_of = lambda c: xc_ref[0, blk(c), GROUP_X:GROUP_X + D_STATE]
        c_of = lambda c: xc_ref[0, blk(c), GROUP_X + D_STATE:GROUP_XBC]
        matrices = lambda c, diff: _ssd_decay_matrices(b_of(c), c_of(c), diff, backward=backward)

        h = h_ref[...]
        diffs = {0: broadcasts(order[0])}
        if nck > 1:
            diffs[1] = broadcasts(order[1])
        pairs = matrices(order[0], diffs.pop(0))
        for n, c in enumerate(order):
            if n + 2 < nck:
                diffs[n + 2] = broadcasts(order[n + 2])
            yd, sn = _ssd_products(xc_ref[0, blk(c), 0:GROUP_X], b_of(c), pairs, ew_ref[blk(c), GROUP_X:])
            if n + 1 < nck:
                pairs = matrices(order[n + 1], diffs.pop(n + 1))
            ch = jnp.dot(c_of(c), h.astype(bf16), preferred_element_type=f32)
            finish(c, yd + ew_ref[blk(c), 0:GROUP_X] * ch)
            decay = ew_ref[c * q + end:c * q + end + 1, 0:GROUP_X]
            h = decay * h + sn
        h_ref[...] = h

    @pl.when(i == 0)
    def _():
        h_ref[...] = jnp.zeros_like(h_ref)

    @pl.when(ph == 0)
    def _forward():
        base = pl.multiple_of(i * lb, lb)

        def finish(c, y):
            yf_ref[pl.ds(base + c * q, q), :] = y

        scan_block(backward=False, finish=finish)

    @pl.when(ph == 1)
    def _backward():
        base = pl.multiple_of((nb - 1 - i) * lb, lb)

        def finish(c, y):
            y = y + yf_ref[pl.ds(base + c * q, q), :] + dskip_ref[0] * xc_ref[0, blk(c), 0:GROUP_X].astype(f32)
            zq = z_ref[0, blk(c), :].astype(f32)
            y = y * (zq * _sigmoid(zq))
            ms = jnp.mean(y * y, axis=-1, keepdims=True)
            y_ref[0, blk(c), :] = (y * lax.rsqrt(ms + NORM_EPS) * normw_ref[0]).astype(y_ref.dtype)

        scan_block(backward=True, finish=finish)


def _ssd(xc, dtt, z, dt_bias_f, dt_bias_b, a_log_f, a_log_b, d_skip, ssd_norm_w, *, lb=2048):
    bsz, s, _ = xc.shape
    nb = s // lb
    assert s % lb == 0 and lb % SSD_CHUNK == 0
    f32 = jnp.float32
    per_group = lambda f, b: jnp.concatenate(
        [f.astype(f32).reshape(SSD_GROUPS, SSD_HEADS_PER_GROUP), b.astype(f32).reshape(SSD_GROUPS, SSD_HEADS_PER_GROUP)],
        axis=1).reshape(SSD_GROUPS, DT_PER_GROUP, 1)
    dtb = per_group(dt_bias_f, dt_bias_b)
    alog = per_group(a_log_f, a_log_b)
    dskip = jnp.repeat(d_skip.astype(f32), SSD_HEAD_DIM).reshape(SSD_GROUPS, 1, GROUP_X)
    normw = ssd_norm_w.astype(f32).reshape(SSD_GROUPS, 1, GROUP_X)
    sel, bdmask = _ssd_select_matrix(), _ssd_blockdiag_mask()

    any_blk = lambda ph, i: jnp.where(ph == 0, i, nb - 1 - i)
    bwd_blk = lambda ph, i: jnp.where(ph == 0, nb - 1, nb - 1 - i)
    par = lambda shape: pl.BlockSpec(shape, lambda b, g, ph, i: (g, 0, 0))
    return pl.pallas_call(
        functools.partial(_ssd_kernel, lb=lb, nb=nb),
        grid=(bsz, SSD_GROUPS, 2, nb),
        in_specs=[
            pl.BlockSpec((1, lb, GROUP_XBC), lambda b, g, ph, i: (b, any_blk(ph, i), g)),
            pl.BlockSpec((1, 1, DT_PER_GROUP, lb), lambda b, g, ph, i: (b, g, 0, any_blk(ph, i))),
            pl.BlockSpec((1, lb, GROUP_X), lambda b, g, ph, i: (b, bwd_blk(ph, i), g)),
            par((1, DT_PER_GROUP, 1)), par((1, DT_PER_GROUP, 1)),
            par((1, 1, GROUP_X)), par((1, 1, GROUP_X)),
            pl.BlockSpec(sel.shape, lambda b, g, ph, i: (0, 0)),
            pl.BlockSpec(bdmask.shape, lambda b, g, ph, i: (0, 0)),
        ],
        out_specs=pl.BlockSpec((1, lb, GROUP_X), lambda b, g, ph, i: (b, bwd_blk(ph, i), g)),
        out_shape=jax.ShapeDtypeStruct((bsz, s, D_INNER), jnp.bfloat16),
        scratch_shapes=[
            pltpu.VMEM((s, GROUP_X), f32),
            pltpu.VMEM((D_STATE, GROUP_X), f32),
            pltpu.VMEM((lb, 2 * GROUP_X), f32),
        ],
        compiler_params=pltpu.CompilerParams(
            dimension_semantics=("arbitrary",) * 4, vmem_limit_bytes=V7X_VMEM_LIMIT),
        name="ssd",
    )(xc, dtt, z, dtb, alog, dskip, normw, sel, bdmask)


ATTN_QB = 128
ATTN_WIN = ATTN_QB + 2 * ATTN_HALF
ATTN_BLOCKS_PER_ITER = 2


def _attn_bias_table():
    heads = len(DILATIONS) * HEADS_PER_PATTERN
    slopes = 2.0 ** (-8.0 * np.arange(1, heads + 1) / heads)
    rel = (np.arange(ATTN_WIN)[None, :] - ATTN_HALF) - np.arange(ATTN_QB)[:, None]
    tabs = []
    for h in range(heads):
        dil = DILATIONS[h // HEADS_PER_PATTERN]
        tabs.append(np.where(np.abs(rel) <= ATTN_HALF, -slopes[h] * np.abs(rel) * dil, NEG_BIG))
    return jnp.asarray(np.stack(tabs), dtype=jnp.float32)


def _attn_kernel(bias_ref, q1_ref, p1_ref, n1_ref, q2_ref, p2_ref, n2_ref, q3_ref, p3_ref, n3_ref, y_ref,
                 kv1_ref, kv2_ref, kv3_ref, o_ref, lse_ref, *, ta, nt, seq):
    i = pl.program_id(1)
    f32, bf16 = jnp.float32, jnp.bfloat16
    lane = lax.broadcasted_iota(jnp.int32, (ATTN_QB, LANES), 1)
    first_half = lane < ATTN_HEAD_DIM
    kcol = lax.broadcasted_iota(jnp.int32, (1, ATTN_WIN), 1)

    for pat, (dil, q_ref, prev_ref, next_ref, kv_ref) in enumerate((
            (DILATIONS[0], q1_ref, p1_ref, n1_ref, kv1_ref),
            (DILATIONS[1], q2_ref, p2_ref, n2_ref, kv2_ref),
            (DILATIONS[2], q3_ref, p3_ref, n3_ref, kv3_ref))):
        n = ta // dil
        seq_l = seq // dil
        kv_ref[:, 0:ATTN_HALF, :] = prev_ref[0, :, :, PATTERN_WIDTH:]
        kv_ref[:, ATTN_HALF:ATTN_HALF + n, :] = q_ref[0, :, :, PATTERN_WIDTH:]
        kv_ref[:, ATTN_HALF + n:2 * ATTN_HALF + n, :] = next_ref[0, :, :, PATTERN_WIDTH:]
        nqb = n // ATTN_QB

        def block(it, carry, dil=dil, q_ref=q_ref, kv_ref=kv_ref, nqb=nqb, n=n, seq_l=seq_l, pat=pat):
            nh = HEADS_PER_PATTERN
            items = range(ATTN_BLOCKS_PER_ITER * nh)
            places, pens, vps, scs = [], [], [], []
            for k in range(ATTN_BLOCKS_PER_ITER):
                idx = it * ATTN_BLOCKS_PER_ITER + k
                r = idx // nqb
                row0 = pl.multiple_of((idx % nqb) * ATTN_QB, ATTN_QB)
                places.append((r, row0))
                kpos = i * n + row0 - ATTN_HALF + kcol
                pens.append(jnp.where((kpos >= 0) & (kpos < seq_l), 0.0, NEG_BIG).astype(f32))
                for pair in range(nh // 2):
                    cs = slice(pair * LANES, (pair + 1) * LANES)
                    qp = q_ref[0, r, pl.ds(row0, ATTN_QB), cs]
                    kp = kv_ref[r, pl.ds(row0, ATTN_WIN), cs]
                    vps.append(kv_ref[r, pl.ds(row0, ATTN_WIN),
                                      pair * LANES + PATTERN_WIDTH:(pair + 1) * LANES + PATTERN_WIDTH])
                    zero = jnp.zeros_like(qp)
                    for qh in (jnp.where(first_half, qp, zero), jnp.where(first_half, zero, qp)):
                        scs.append(lax.dot_general(qh, kp, (((1,), (1,)), ((), ())), preferred_element_type=f32))
            scs = [scs[t] + bias_ref[pat * nh + t % nh] + pens[t // nh] for t in items]
            ms = [jnp.max(scs[t], axis=-1, keepdims=True) for t in items]
            pes = [jnp.exp(scs[t] - ms[t]) for t in items]
            dens = [jnp.sum(pes[t], axis=-1, keepdims=True) for t in items]
            pvs = [jnp.dot(pes[t].astype(bf16), vps[t // 2], preferred_element_type=f32) for t in items]
            outs = [pvs[t] / dens[t] for t in items]
            lses = [ms[t] + jnp.log(dens[t]) for t in items]
            for k, (r, row0) in enumerate(places):
                rows = pl.ds(r + row0 * dil, ATTN_QB, stride=dil)
                for pair in range(nh // 2):
                    t = k * nh + 2 * pair
                    o_ref[pat, pair, rows, :] = jnp.where(first_half, outs[t], outs[t + 1])
                    lse_ref[pat, pair, rows, :] = jnp.where(first_half, lses[t], lses[t + 1])
            return carry

        lax.fori_loop(0, dil * nqb // ATTN_BLOCKS_PER_ITER, block, 0)

    for pair in range(HEADS_PER_PATTERN // 2):
        l0, l1, l2 = lse_ref[0, pair], lse_ref[1, pair], lse_ref[2, pair]
        m = jnp.maximum(jnp.maximum(l0, l1), l2)
        w0, w1, w2 = jnp.exp(l0 - m), jnp.exp(l1 - m), jnp.exp(l2 - m)
        y = (w0 * o_ref[0, pair] + w1 * o_ref[1, pair] + w2 * o_ref[2, pair]) / (w0 + w1 + w2)
        y_ref[0, :, pair * LANES:(pair + 1) * LANES] = y.astype(y_ref.dtype)


def _attention(qkv1, qkv2, qkv3, *, ta=2048):
    bsz, s, _ = qkv1.shape
    nt = s // ta
    assert s % ta == 0 and ta % (ATTN_QB * DILATIONS[2]) == 0 and (ta // ATTN_QB) % ATTN_BLOCKS_PER_ITER == 0
    qkv1 = qkv1.reshape(bsz, 1, s, QKV_WIDTH)
    hb = ATTN_HALF
    specs = []
    for dil in DILATIONS:
        n = ta // dil
        last = s // dil // hb - 1
        specs += [
            pl.BlockSpec((1, dil, n, QKV_WIDTH), lambda b, i: (b, 0, i, 0)),
            pl.BlockSpec((1, dil, hb, QKV_WIDTH), lambda b, i, n=n: (b, 0, jnp.maximum(i * (n // hb) - 1, 0), 0)),
            pl.BlockSpec((1, dil, hb, QKV_WIDTH),
                         lambda b, i, n=n, last=last: (b, 0, jnp.minimum((i + 1) * (n // hb), last), 0)),
        ]
    bias = _attn_bias_table()
    kvw = 2 * PATTERN_WIDTH
    return pl.pallas_call(
        functools.partial(_attn_kernel, ta=ta, nt=nt, seq=s),
        grid=(bsz, nt),
        in_specs=[pl.BlockSpec(bias.shape, lambda b, i: (0, 0, 0))] + specs,
        out_specs=pl.BlockSpec((1, ta, PATTERN_WIDTH), lambda b, i: (b, i, 0)),
        out_shape=jax.ShapeDtypeStruct((bsz, s, PATTERN_WIDTH), jnp.bfloat16),
        scratch_shapes=[pltpu.VMEM((dil, ta // dil + 2 * hb, kvw), jnp.bfloat16) for dil in DILATIONS] + [
            pltpu.VMEM((len(DILATIONS), PATTERN_WIDTH // LANES, ta, LANES), jnp.float32),
            pltpu.VMEM((len(DILATIONS), PATTERN_WIDTH // LANES, ta, LANES), jnp.float32),
        ],
        compiler_params=pltpu.CompilerParams(
            dimension_semantics=("arbitrary", "arbitrary"), vmem_limit_bytes=V7X_VMEM_LIMIT),
        name="dilated_attn",
    )(bias, qkv1, qkv1, qkv1, qkv2, qkv2, qkv2, qkv3, qkv3, qkv3)


def _layer_norm(v, g, b):
    mu = jnp.mean(v, axis=-1, keepdims=True)
    c = v - mu
    var = jnp.mean(c * c, axis=-1, keepdims=True)
    return c * lax.rsqrt(var + NORM_EPS) * g + b


def _merge_mlp_kernel(x_ref, yssd_ref, yatt_ref, gate_ref, bgate_ref, wps_ref, wpa_ref, wout_ref,
                      ln1g_ref, ln1b_ref, wup_ref, wdown_ref, ln2g_ref, ln2b_ref, o_ref, *, ffc):
    f32, bf16 = jnp.float32, jnp.bfloat16
    tm = x_ref.shape[1]

    def mixed(rs):
        ys = jnp.dot(yssd_ref[0, rs, :], wps_ref[...], preferred_element_type=f32)
        ya = jnp.dot(yatt_ref[0, rs, :], wpa_ref[...], preferred_element_type=f32)
        g_ssd = _sigmoid(gate_ref[0, rs, 0:D_MODEL].astype(f32) + bgate_ref[:, 0:D_MODEL])
        g_att = _sigmoid(gate_ref[0, rs, D_MODEL:2 * D_MODEL].astype(f32) + bgate_ref[:, D_MODEL:2 * D_MODEL])
        return jnp.dot((g_ssd * ys + g_att * ya).astype(bf16), wout_ref[...], preferred_element_type=f32)

    def norm1(rs, mix):
        return _layer_norm(ALPHA * x_ref[0, rs, :] + mix, ln1g_ref[...], ln1b_ref[...])

    def mlp(h):
        hb = h.astype(bf16)
        f = None
        for c in range(0, D_FF, ffc):
            up = jnp.maximum(jnp.dot(hb, wup_ref[:, c:c + ffc], preferred_element_type=f32), 0.0)
            part = jnp.dot((up * up).astype(bf16), wdown_ref[c:c + ffc, :], preferred_element_type=f32)
            f = part if f is None else f + part
        return f

    def norm2(rs, h, f):
        o_ref[0, rs, :] = _layer_norm(ALPHA * h + f, ln2g_ref[...], ln2b_ref[...]).astype(o_ref.dtype)

    ra, rb = slice(0, tm // 2), slice(tm // 2, tm)
    mix_a = mixed(ra)
    h_a = norm1(ra, mix_a)
    mix_b = mixed(rb)
    h_b = norm1(rb, mix_b)
    f_a = mlp(h_a)
    norm2(ra, h_a, f_a)
    f_b = mlp(h_b)
    norm2(rb, h_b, f_b)


def _merge_mlp(x, y_ssd, y_att, gate, b_gate, w_proj_ssd, w_proj_attn, w_out, ln1_g, ln1_b, w_up, w_down,
               ln2_g, ln2_b, *, tm=512, ffc=1024):
    bsz, s, d = x.shape
    assert s % tm == 0 and D_FF % ffc == 0
    bf16, f32 = jnp.bfloat16, jnp.float32
    row = lambda v: v.astype(f32).reshape(1, -1)
    tok = lambda width: pl.BlockSpec((1, tm, width), lambda b, i: (b, i, 0))
    resident = lambda shape: pl.BlockSpec(shape, lambda b, i: (0,) * len(shape), pipeline_mode=pl.Buffered(1))
    operands = [
        (x, tok(d)), (y_ssd, tok(D_INNER)), (y_att, tok(PATTERN_WIDTH)), (gate, tok(2 * D_MODEL)),
        (row(b_gate), None), (w_proj_ssd.astype(bf16), None), (w_proj_attn.astype(bf16), None),
        (w_out.astype(bf16), None), (row(ln1_g), None), (row(ln1_b), None),
        (w_up.astype(bf16), None), (w_down.astype(bf16), None), (row(ln2_g), None), (row(ln2_b), None),
    ]
    args = [a for a, _ in operands]
    in_specs = [spec if spec is not None else resident(a.shape) for a, spec in operands]
    return pl.pallas_call(
        functools.partial(_merge_mlp_kernel, ffc=ffc),
        grid=(bsz, s // tm),
        in_specs=in_specs,
        out_specs=tok(d),
        out_shape=jax.ShapeDtypeStruct((bsz, s, d), x.dtype),
        compiler_params=pltpu.CompilerParams(
            dimension_semantics=("arbitrary", "arbitrary"), vmem_limit_bytes=V7X_VMEM_LIMIT),
        name="merge_mlp",
    )(*args)


def kernel(x, w_in, b_gate, conv_w, conv_b, dt_bias_f, dt_bias_b, a_log_f, a_log_b, d_skip, ssd_norm_w,
           w_proj_ssd, w_proj_attn, w_out, ln1_g, ln1_b, w_up, w_down, ln2_g, ln2_b):
    w_packed, wdt_t = _pack_w_in(w_in)
    z, xc, qkv1, qkv2, qkv3, gate, dtt = _in_proj(x, w_packed, wdt_t, conv_w, conv_b)
    y_ssd = _ssd(xc, dtt, z, dt_bias_f, dt_bias_b, a_log_f, a_log_b, d_skip, ssd_norm_w)
    y_att = _attention(qkv1, qkv2, qkv3)
    return _merge_mlp(x, y_ssd, y_att, gate, b_gate, w_proj_ssd, w_proj_attn, w_out, ln1_g, ln1_b,
                      w_up, w_down, ln2_g, ln2_b)
```

```python
import functools
import math

import jax
import jax.numpy as jnp
import numpy as np
from jax import lax
from jax.experimental import pallas as pl
from jax.experimental.pallas import tpu as pltpu

D_MODEL = 1024
D_INNER = 2048
SSD_HEAD_DIM = 64
SSD_GROUPS = 4
SSD_HEADS_PER_GROUP = 8
D_STATE = 128
D_CONV = 5
SSD_CHUNK = 128
GROUP_X = SSD_HEADS_PER_GROUP * SSD_HEAD_DIM
GROUP_XBC = GROUP_X + 2 * D_STATE
CONV_DIM = D_INNER + 2 * SSD_GROUPS * D_STATE
NORM_EPS = 1e-5
ATTN_HEAD_DIM = 64
DILATIONS = (1, 4, 16)
ATTN_HALF = 64
HEADS_PER_PATTERN = 4
PATTERN_WIDTH = HEADS_PER_PATTERN * ATTN_HEAD_DIM
QKV_WIDTH = 3 * PATTERN_WIDTH
ATTN_WIDTH = 3 * PATTERN_WIDTH
D_FF = 4 * D_MODEL
DT_PER_GROUP = 2 * SSD_HEADS_PER_GROUP
DT_ROWS = SSD_GROUPS * DT_PER_GROUP
ALPHA = 2.0 ** 0.25
NEG_BIG = -1e30
LOG2E = 1.4426950408889634

V7X_VMEM_LIMIT = 56 * 1024 * 1024
LANES = 128
SUBLANES = 8
BF16_SUBLANES = 16
CONV_ROW_STRIDE = 4

_OFF_Z = 0
_OFF_XBC = _OFF_Z + D_INNER
_OFF_QKV = _OFF_XBC + CONV_DIM
_OFF_GATE = _OFF_QKV + 3 * QKV_WIDTH
IN_COLS_PACKED = _OFF_GATE + 2 * D_MODEL


def _group_xbc(a):
    nb = SSD_GROUPS * D_STATE
    xs, bs, cs = a[..., :D_INNER], a[..., D_INNER:D_INNER + nb], a[..., D_INNER + nb:]
    parts = []
    for g in range(SSD_GROUPS):
        parts += [xs[..., g * GROUP_X:(g + 1) * GROUP_X], bs[..., g * D_STATE:(g + 1) * D_STATE],
                  cs[..., g * D_STATE:(g + 1) * D_STATE]]
    return jnp.concatenate(parts, axis=-1)


def _pack_w_in(w_in):
    splits = np.cumsum([D_INNER, CONV_DIM, 32, 32, ATTN_WIDTH, ATTN_WIDTH, ATTN_WIDTH])
    wz, wxbc, wdtf, wdtb, wq, wk, wv, wg = jnp.split(w_in, list(splits), axis=1)
    wxbc = _group_xbc(wxbc)
    wq = wq * (1.0 / math.sqrt(ATTN_HEAD_DIM))
    qkv = []
    for p in range(3):
        s = slice(p * PATTERN_WIDTH, (p + 1) * PATTERN_WIDTH)
        qkv += [wq[:, s], wk[:, s], wv[:, s]]
    dts = []
    for g in range(SSD_GROUPS):
        s = slice(g * SSD_HEADS_PER_GROUP, (g + 1) * SSD_HEADS_PER_GROUP)
        dts += [wdtf[:, s], wdtb[:, s]]
    packed = jnp.concatenate([wz, wxbc] + qkv + [wg], axis=1).astype(jnp.bfloat16)
    wdt_t = jnp.concatenate(dts, axis=1).T.astype(jnp.bfloat16)
    return packed, wdt_t


def _sigmoid(v):
    return 0.5 * (1.0 + jnp.tanh(0.5 * v))


def _in_proj_kernel(x_ref, xprev_ref, xnext_ref, w_ref, wdt_t_ref, convw_ref, convb_ref,
                    z_ref, xbc_ref, qkv1_ref, qkv2_ref, qkv3_ref, gate_ref, dtt_ref,
                    xb_ref, cin_ref, cout_ref, stage_ref, *, tm, nc):
    i = pl.program_id(1)
    f32, bf16 = jnp.float32, jnp.bfloat16
    halo = BF16_SUBLANES
    pad = D_CONV // 2
    xb_ref[0:halo, :] = jnp.where(i > 0, xprev_ref[0], 0.0).astype(bf16)
    xb_ref[halo:halo + tm, :] = x_ref[0].astype(bf16)
    xb_ref[halo + tm:2 * halo + tm, :] = jnp.where(i < pl.num_programs(1) - 1, xnext_ref[0], 0.0).astype(bf16)

    def proj(off, width):
        return jnp.dot(xb_ref[halo:halo + tm, :], w_ref[:, off:off + width], preferred_element_type=f32)

    nslab = nc // LANES

    def store(out_ref, off, c):
        def run():
            out_ref[0, :, c:c + nc] = proj(off + c, nc).astype(out_ref.dtype)
        return run

    def qkv_dilated(out_ref, pat, dil, u):
        def run():
            res = proj(_OFF_QKV + pat * QKV_WIDTH + u * nc, nc)
            for c in range(nslab):
                stage_ref[u * nslab + c] = res[:, c * LANES:(c + 1) * LANES]
            for r in range(dil):
                for c in range(nslab):
                    out_ref[0, r, :, u * nc + c * LANES:u * nc + (c + 1) * LANES] = (
                        stage_ref[u * nslab + c, pl.ds(r, tm // dil, stride=dil), :].astype(out_ref.dtype))
        return run

    def dt_rows():
        dtt = lax.dot_general(wdt_t_ref[...], xb_ref[halo:halo + tm, :], (((1,), (1,)), ((), ())),
                              preferred_element_type=f32)
        for g in range(SSD_GROUPS):
            dtt_ref[0, g] = dtt[g * DT_PER_GROUP:(g + 1) * DT_PER_GROUP, :]

    def conv_slab(buf, s, cs):
        taps = [jnp.broadcast_to(convw_ref[k:k + 1, cs], (SUBLANES, LANES)) for k in range(D_CONV)]
        bias = jnp.broadcast_to(convb_ref[:, cs], (SUBLANES, LANES))
        for r0 in range(0, tm, SUBLANES * CONV_ROW_STRIDE):
            for v in range(CONV_ROW_STRIDE):
                acc = bias
                for k in range(D_CONV):
                    lo = halo - pad + k + r0 + v
                    acc = acc + taps[k] * cin_ref[buf, s, pl.ds(lo, SUBLANES, stride=CONV_ROW_STRIDE), :]
                cout_ref[buf, s, pl.ds(r0 + v, SUBLANES, stride=CONV_ROW_STRIDE), :] = acc * _sigmoid(acc)
        xbc_ref[0, :, cs] = cout_ref[buf, s].astype(xbc_ref.dtype)

    others = ([store(z_ref, _OFF_Z, c) for c in range(0, D_INNER, nc)]
              + [store(gate_ref, _OFF_GATE, c) for c in range(0, 2 * D_MODEL, nc)]
              + [store(qkv1_ref, _OFF_QKV, c) for c in range(0, QKV_WIDTH, nc)]
              + [qkv_dilated(qkv2_ref, 1, DILATIONS[1], u) for u in range(QKV_WIDTH // nc)]
              + [qkv_dilated(qkv3_ref, 2, DILATIONS[2], u) for u in range(QKV_WIDTH // nc)] + [dt_rows])
    nchunk = CONV_DIM // nc
    for c in range(nchunk):
        buf = c % 2
        ext = jnp.dot(xb_ref[...], w_ref[:, _OFF_XBC + c * nc:_OFF_XBC + (c + 1) * nc],
                      preferred_element_type=f32)
        for s in range(nslab):
            cin_ref[buf, s] = ext[:, s * LANES:(s + 1) * LANES]
        for s in range(nslab):
            if others:
                others.pop(0)()
            conv_slab(buf, s, slice(c * nc + s * LANES, c * nc + (s + 1) * LANES))
    for job in others:
        job()


def _in_proj(x, w_packed, wdt_t, conv_w, conv_b, *, tm=512, nc=256):
    bsz, s, d = x.shape
    assert s % tm == 0 and tm % (BF16_SUBLANES * DILATIONS[2]) == 0 and CONV_DIM % nc == 0
    act = jnp.bfloat16
    halo = BF16_SUBLANES
    hb = tm // halo
    convw = _group_xbc(conv_w.astype(jnp.float32))
    convb = _group_xbc(conv_b.astype(jnp.float32)).reshape(1, CONV_DIM)
    out_shape = (
        jax.ShapeDtypeStruct((bsz, s, D_INNER), act),
        jax.ShapeDtypeStruct((bsz, s, CONV_DIM), act),
        jax.ShapeDtypeStruct((bsz, s, QKV_WIDTH), act),
        jax.ShapeDtypeStruct((bsz, DILATIONS[1], s // DILATIONS[1], QKV_WIDTH), act),
        jax.ShapeDtypeStruct((bsz, DILATIONS[2], s // DILATIONS[2], QKV_WIDTH), act),
        jax.ShapeDtypeStruct((bsz, s, 2 * D_MODEL), act),
        jax.ShapeDtypeStruct((bsz, SSD_GROUPS, DT_PER_GROUP, s), jnp.float32),
    )
    tok = lambda width: pl.BlockSpec((1, tm, width), lambda b, i: (b, i, 0))
    resident = lambda shape: pl.BlockSpec(shape, lambda b, i: (0,) * len(shape),
                                          pipeline_mode=pl.Buffered(1))
    return pl.pallas_call(
        functools.partial(_in_proj_kernel, tm=tm, nc=nc),
        grid=(bsz, s // tm),
        in_specs=[
            tok(d),
            pl.BlockSpec((1, halo, d), lambda b, i: (b, jnp.maximum(i * hb - 1, 0), 0)),
            pl.BlockSpec((1, halo, d), lambda b, i: (b, jnp.minimum((i + 1) * hb, s // halo - 1), 0)),
            resident(w_packed.shape), resident(wdt_t.shape), resident(convw.shape), resident(convb.shape),
        ],
        out_specs=(
            tok(D_INNER), tok(CONV_DIM), tok(QKV_WIDTH),
            pl.BlockSpec((1, DILATIONS[1], tm // DILATIONS[1], QKV_WIDTH), lambda b, i: (b, 0, i, 0)),
            pl.BlockSpec((1, DILATIONS[2], tm // DILATIONS[2], QKV_WIDTH), lambda b, i: (b, 0, i, 0)),
            tok(2 * D_MODEL),
            pl.BlockSpec((1, SSD_GROUPS, DT_PER_GROUP, tm), lambda b, i: (b, 0, 0, i)),
        ),
        out_shape=out_shape,
        scratch_shapes=[
            pltpu.VMEM((tm + 2 * halo, d), jnp.bfloat16),
            pltpu.VMEM((2, nc // LANES, tm + 2 * halo, LANES), jnp.float32),
            pltpu.VMEM((2, nc // LANES, tm, LANES), jnp.float32),
            pltpu.VMEM((QKV_WIDTH // LANES, tm, LANES), jnp.float32),
        ],
        compiler_params=pltpu.CompilerParams(
            dimension_semantics=("arbitrary", "arbitrary"), vmem_limit_bytes=V7X_VMEM_LIMIT),
        name="in_proj",
    )(x, x, x, w_packed, wdt_t, convw, convb)


def _split3_bf16(a):
    hi = a.astype(jnp.bfloat16)
    r1 = a - hi.astype(jnp.float32)
    mid = r1.astype(jnp.bfloat16)
    lo = (r1 - mid.astype(jnp.float32)).astype(jnp.bfloat16)
    return hi, mid, lo


_ROW_CUM, _ROW_E, _ROW_W = 32, 56, 80
_STACK_PARTS = 3
_BCAST_COLS = SSD_HEADS_PER_GROUP * SSD_CHUNK + 2 * GROUP_X
LOG2_DT_FLOOR = -1e4


def _ssd_select_matrix():
    r = np.zeros((SSD_CHUNK, _BCAST_COLS), np.float32)
    hd, q = SSD_HEADS_PER_GROUP, SSD_CHUNK
    for h in range(hd):
        for part in range(_STACK_PARTS):
            r[_ROW_CUM + part * hd + h, h * q:(h + 1) * q] = 1.0
            e0 = hd * q + h * SSD_HEAD_DIM
            r[_ROW_E + part * hd + h, e0:e0 + SSD_HEAD_DIM] = 1.0
            r[_ROW_W + part * hd + h, e0 + GROUP_X:e0 + GROUP_X + SSD_HEAD_DIM] = 1.0
    return jnp.asarray(r, dtype=jnp.bfloat16)


def _ssd_blockdiag_mask():
    hd, q = SSD_HEADS_PER_GROUP, SSD_CHUNK
    m = np.zeros((_ROW_CUM, hd * q), np.float32)
    for part in range(_STACK_PARTS):
        for h in range(hd):
            m[part * hd + h, h * q:(h + 1) * q] = 1.0
    return jnp.asarray(m)


def _ssd_decay_terms(dt_raw_t, dt_bias_c, a_log_c, bdmask_ref, *, backward):
    q, hd = SSD_CHUNK, SSD_HEADS_PER_GROUP
    f32, bf16 = jnp.float32, jnp.bfloat16
    row = lax.broadcasted_iota(jnp.int32, (q, q), 0)
    col = lax.broadcasted_iota(jnp.int32, (q, q), 1)
    tri = ((row >= col) if backward else (row <= col)).astype(f32).astype(bf16)
    end = 0 if backward else q - 1
    v = dt_raw_t + dt_bias_c
    dt = jnp.maximum(v, 0.0) + jnp.log1p(jnp.exp(-jnp.abs(v)))
    a2 = dt * (-jnp.exp(a_log_c) * LOG2E)
    cum3 = jnp.dot(jnp.concatenate(_split3_bf16(a2), axis=0), tri, preferred_element_type=f32)
    cum2 = cum3[0:hd] + cum3[hd:2 * hd] + cum3[2 * hd:3 * hd]
    tot2 = cum2[:, end:end + 1]
    e_r = jnp.exp2(cum2)
    w_r = dt * jnp.exp2(tot2 - cum2)
    nrowp = jnp.maximum(jnp.log2(dt), LOG2_DT_FLOOR) - cum2
    stacked = jnp.concatenate((jnp.ones((_ROW_CUM, q), bf16),) + _split3_bf16(cum2) + _split3_bf16(e_r)
                              + _split3_bf16(w_r) + (jnp.zeros((q - _ROW_W - _STACK_PARTS * hd, q), bf16),),
                              axis=0)
    cols = stacked.astype(f32).T.astype(bf16)
    nparts = jnp.concatenate(_split3_bf16(nrowp) + (jnp.zeros((hd, q), bf16),), axis=0).astype(f32)
    data_rows = (jnp.tile(nparts, (1, hd)) * bdmask_ref[...]).astype(bf16)
    return cols, data_rows


def _ssd_decay_matrices(bq, cq, big_diff, *, backward):
    q, hd = SSD_CHUNK, SSD_HEADS_PER_GROUP
    f32, bf16 = jnp.float32, jnp.bfloat16
    row = lax.broadcasted_iota(jnp.int32, (q, q), 0)
    col = lax.broadcasted_iota(jnp.int32, (q, q), 1)
    mask = (row <= col) if backward else (row >= col)
    cb = lax.dot_general(cq, bq, (((1,), (1,)), ((), ())), preferred_element_type=f32)
    pairs = []
    for j in range(hd // 2):
        ms = [(cb * jnp.exp2(jnp.where(mask, big_diff[:, h * q:(h + 1) * q], NEG_BIG))).astype(bf16)
              for h in (2 * j, 2 * j + 1)]
        pairs.append(jnp.concatenate(ms, axis=1))
    return pairs


def _ssd_products(xq, bq, pairs, w_exp):
    q = SSD_CHUNK
    f32, bf16 = jnp.float32, jnp.bfloat16
    lane = lax.broadcasted_iota(jnp.int32, (q, LANES), 1)
    first_half = lane < SSD_HEAD_DIM
    ys = []
    for j, lhs in enumerate(pairs):
        xpair = xq[:, j * LANES:(j + 1) * LANES]
        zero = jnp.zeros_like(xpair)
        rhs = jnp.concatenate([jnp.where(first_half, xpair, zero),
                               jnp.where(first_half, zero, xpair)], axis=0)
        ys.append(jnp.dot(lhs, rhs, preferred_element_type=f32))
    xw = xq * w_exp.astype(bf16)
    s_new = lax.dot_general(bq, xw, (((0,), (0,)), ((), ())), preferred_element_type=f32)
    return jnp.concatenate(ys, axis=1), s_new


def _ssd_kernel(xc_ref, dtt_ref, z_ref, dtb_ref, alog_ref, dskip_ref, normw_ref, sel_ref, bdmask_ref, y_ref,
                yf_ref, h_ref, ew_ref, *, lb, nb):
    ph = pl.program_id(2)
    i = pl.program_id(3)
    q = SSD_CHUNK
    hd = SSD_HEADS_PER_GROUP
    nck = lb // q
    f32, bf16 = jnp.float32, jnp.bfloat16
    blk = lambda c: slice(c * q, (c + 1) * q)

    def scan_block(*, backward, finish):
        hs = slice(hd, DT_PER_GROUP) if backward else slice(0, hd)
        order = list(reversed(range(nck))) if backward else list(range(nck))
        end = 0 if backward else q - 1
        terms = [_ssd_decay_terms(dtt_ref[0, 0, hs, blk(c)], dtb_ref[0, hs, :], alog_ref[0, hs, :],
                                  bdmask_ref, backward=backward) for c in range(nck)]

        def broadcasts(c):
            cols, data_rows = terms[c]
            ew_ref[blk(c), :] = jnp.dot(cols, sel_ref[:, hd * q:], preferred_element_type=f32)
            r_diff = jnp.concatenate([data_rows, sel_ref[_ROW_CUM:, 0:hd * q]], axis=0)
            return jnp.dot(cols, r_diff, preferred_element_type=f32)

        b_of = lambda c: xc_ref[0, blk(c), GROUP_X:GROUP_X + D_STATE]
        c_of = lambda c: xc_ref[0, blk(c), GROUP_X + D_STATE:GROUP_XBC]
        matrices = lambda c, diff: _ssd_decay_matrices(b_of(c), c_of(c), diff, backward=backward)

        h = h_ref[...]
        diffs = {0: broadcasts(order[0])}
        if nck > 1:
            diffs[1] = broadcasts(order[1])
        pairs = matrices(order[0], diffs.pop(0))
        for n, c in enumerate(order):
            if n + 2 < nck:
                diffs[n + 2] = broadcasts(order[n + 2])
            yd, sn = _ssd_products(xc_ref[0, blk(c), 0:GROUP_X], b_of(c), pairs, ew_ref[blk(c), GROUP_X:])
            if n + 1 < nck:
                pairs = matrices(order[n + 1], diffs.pop(n + 1))
            ch = jnp.dot(c_of(c), h.astype(bf16), preferred_element_type=f32)
            finish(c, yd + ew_ref[blk(c), 0:GROUP_X] * ch)
            decay = ew_ref[c * q + end:c * q + end + 1, 0:GROUP_X]
            h = decay * h + sn
        h_ref[...] = h

    @pl.when(i == 0)
    def _():
        h_ref[...] = jnp.zeros_like(h_ref)

    @pl.when(ph == 0)
    def _forward():
        base = pl.multiple_of(i * lb, lb)

        def finish(c, y):
            yf_ref[pl.ds(base + c * q, q), :] = y + dskip_ref[0] * xc_ref[0, blk(c), 0:GROUP_X].astype(f32)

        scan_block(backward=False, finish=finish)

    @pl.when(ph == 1)
    def _backward():
        base = pl.multiple_of((nb - 1 - i) * lb, lb)

        def finish(c, y):
            y = y + yf_ref[pl.ds(base + c * q, q), :]
            zq = z_ref[0, blk(c), :].astype(f32)
            y = y * (zq * _sigmoid(zq))
            ms = jnp.mean(y * y, axis=-1, keepdims=True)
            y_ref[0, blk(c), :] = (y * lax.rsqrt(ms + NORM_EPS) * normw_ref[0]).astype(y_ref.dtype)

        scan_block(backward=True, finish=finish)


def _ssd(xc, dtt, z, dt_bias_f, dt_bias_b, a_log_f, a_log_b, d_skip, ssd_norm_w, *, lb=2048):
    bsz, s, _ = xc.shape
    nb = s // lb
    assert s % lb == 0 and lb % SSD_CHUNK == 0
    f32 = jnp.float32
    per_group = lambda f, b: jnp.concatenate(
        [f.astype(f32).reshape(SSD_GROUPS, SSD_HEADS_PER_GROUP), b.astype(f32).reshape(SSD_GROUPS, SSD_HEADS_PER_GROUP)],
        axis=1).reshape(SSD_GROUPS, DT_PER_GROUP, 1)
    dtb = per_group(dt_bias_f, dt_bias_b)
    alog = per_group(a_log_f, a_log_b)
    dskip = jnp.repeat(d_skip.astype(f32), SSD_HEAD_DIM).reshape(SSD_GROUPS, 1, GROUP_X)
    normw = ssd_norm_w.astype(f32).reshape(SSD_GROUPS, 1, GROUP_X)
    sel, bdmask = _ssd_select_matrix(), _ssd_blockdiag_mask()

    any_blk = lambda ph, i: jnp.where(ph == 0, i, nb - 1 - i)
    bwd_blk = lambda ph, i: jnp.where(ph == 0, nb - 1, nb - 1 - i)
    par = lambda shape: pl.BlockSpec(shape, lambda b, g, ph, i: (g, 0, 0))
    return pl.pallas_call(
        functools.partial(_ssd_kernel, lb=lb, nb=nb),
        grid=(bsz, SSD_GROUPS, 2, nb),
        in_specs=[
            pl.BlockSpec((1, lb, GROUP_XBC), lambda b, g, ph, i: (b, any_blk(ph, i), g)),
            pl.BlockSpec((1, 1, DT_PER_GROUP, lb), lambda b, g, ph, i: (b, g, 0, any_blk(ph, i))),
            pl.BlockSpec((1, lb, GROUP_X), lambda b, g, ph, i: (b, bwd_blk(ph, i), g)),
            par((1, DT_PER_GROUP, 1)), par((1, DT_PER_GROUP, 1)),
            par((1, 1, GROUP_X)), par((1, 1, GROUP_X)),
            pl.BlockSpec(sel.shape, lambda b, g, ph, i: (0, 0)),
            pl.BlockSpec(bdmask.shape, lambda b, g, ph, i: (0, 0)),
        ],
        out_specs=pl.BlockSpec((1, lb, GROUP_X), lambda b, g, ph, i: (b, bwd_blk(ph, i), g)),
        out_shape=jax.ShapeDtypeStruct((bsz, s, D_INNER), jnp.bfloat16),
        scratch_shapes=[
            pltpu.VMEM((s, GROUP_X), f32),
            pltpu.VMEM((D_STATE, GROUP_X), f32),
            pltpu.VMEM((lb, 2 * GROUP_X), f32),
        ],
        compiler_params=pltpu.CompilerParams(
            dimension_semantics=("arbitrary",) * 4, vmem_limit_bytes=V7X_VMEM_LIMIT),
        name="ssd",
    )(xc, dtt, z, dtb, alog, dskip, normw, sel, bdmask)


ATTN_QB = 128
ATTN_WIN = ATTN_QB + 2 * ATTN_HALF
ATTN_BLOCKS_PER_ITER = 4


def _attn_bias_table():
    heads = len(DILATIONS) * HEADS_PER_PATTERN
    slopes = 2.0 ** (-8.0 * np.arange(1, heads + 1) / heads)
    rel = (np.arange(ATTN_WIN)[None, :] - ATTN_HALF) - np.arange(ATTN_QB)[:, None]
    tabs = []
    for h in range(heads):
        dil = DILATIONS[h // HEADS_PER_PATTERN]
        tabs.append(np.where(np.abs(rel) <= ATTN_HALF, -slopes[h] * np.abs(rel) * dil, NEG_BIG))
    return jnp.asarray(np.stack(tabs), dtype=jnp.float32)


def _attn_kernel(bias_ref, q1_ref, p1_ref, n1_ref, q2_ref, p2_ref, n2_ref, q3_ref, p3_ref, n3_ref, y_ref,
                 kv1_ref, kv2_ref, kv3_ref, o_ref, lse_ref, *, ta, nt, seq):
    i = pl.program_id(1)
    f32, bf16 = jnp.float32, jnp.bfloat16
    lane = lax.broadcasted_iota(jnp.int32, (ATTN_QB, LANES), 1)
    first_half = lane < ATTN_HEAD_DIM
    kcol = lax.broadcasted_iota(jnp.int32, (1, ATTN_WIN), 1)

    for pat, (dil, q_ref, prev_ref, next_ref, kv_ref) in enumerate((
            (DILATIONS[0], q1_ref, p1_ref, n1_ref, kv1_ref),
            (DILATIONS[1], q2_ref, p2_ref, n2_ref, kv2_ref),
            (DILATIONS[2], q3_ref, p3_ref, n3_ref, kv3_ref))):
        n = ta // dil
        seq_l = seq // dil
        kv_ref[:, 0:ATTN_HALF, :] = prev_ref[0, :, :, PATTERN_WIDTH:]
        kv_ref[:, ATTN_HALF:ATTN_HALF + n, :] = q_ref[0, :, :, PATTERN_WIDTH:]
        kv_ref[:, ATTN_HALF + n:2 * ATTN_HALF + n, :] = next_ref[0, :, :, PATTERN_WIDTH:]
        nqb = n // ATTN_QB

        def block(it, carry, dil=dil, q_ref=q_ref, kv_ref=kv_ref, nqb=nqb, n=n, seq_l=seq_l, pat=pat):
            nh = HEADS_PER_PATTERN
            items = range(ATTN_BLOCKS_PER_ITER * nh)
            places, pens, vps, scs = [], [], [], []
            for k in range(ATTN_BLOCKS_PER_ITER):
                idx = it * ATTN_BLOCKS_PER_ITER + k
                r = idx // nqb
                row0 = pl.multiple_of((idx % nqb) * ATTN_QB, ATTN_QB)
                places.append((r, row0))
                kpos = i * n + row0 - ATTN_HALF + kcol
                pens.append(jnp.where((kpos >= 0) & (kpos < seq_l), 0.0, NEG_BIG).astype(f32))
                for pair in range(nh // 2):
                    cs = slice(pair * LANES, (pair + 1) * LANES)
                    qp = q_ref[0, r, pl.ds(row0, ATTN_QB), cs]
                    kp = kv_ref[r, pl.ds(row0, ATTN_WIN), cs]
                    vps.append(kv_ref[r, pl.ds(row0, ATTN_WIN),
                                      pair * LANES + PATTERN_WIDTH:(pair + 1) * LANES + PATTERN_WIDTH])
                    zero = jnp.zeros_like(qp)
                    for qh in (jnp.where(first_half, qp, zero), jnp.where(first_half, zero, qp)):
                        scs.append(lax.dot_general(qh, kp, (((1,), (1,)), ((), ())), preferred_element_type=f32))
            scs = [scs[t] + bias_ref[pat * nh + t % nh] + pens[t // nh] for t in items]
            ms = [jnp.max(scs[t], axis=-1, keepdims=True) for t in items]
            pes = [jnp.exp(scs[t] - ms[t]) for t in items]
            dens = [jnp.sum(pes[t], axis=-1, keepdims=True) for t in items]
            pvs = [jnp.dot(pes[t].astype(bf16), vps[t // 2], preferred_element_type=f32) for t in items]
            outs = [pvs[t] / dens[t] for t in items]
            lses = [ms[t] + jnp.log(dens[t]) for t in items]
            for k, (r, row0) in enumerate(places):
                rows = pl.ds(r + row0 * dil, ATTN_QB, stride=dil)
                for pair in range(nh // 2):
                    t = k * nh + 2 * pair
                    o_ref[pat, pair, rows, :] = jnp.where(first_half, outs[t], outs[t + 1])
                    lse_ref[pat, pair, rows, :] = jnp.where(first_half, lses[t], lses[t + 1])
            return carry

        lax.fori_loop(0, dil * nqb // ATTN_BLOCKS_PER_ITER, block, 0)

    for pair in range(HEADS_PER_PATTERN // 2):
        l0, l1, l2 = lse_ref[0, pair], lse_ref[1, pair], lse_ref[2, pair]
        m = jnp.maximum(jnp.maximum(l0, l1), l2)
        w0, w1, w2 = jnp.exp(l0 - m), jnp.exp(l1 - m), jnp.exp(l2 - m)
        y = (w0 * o_ref[0, pair] + w1 * o_ref[1, pair] + w2 * o_ref[2, pair]) / (w0 + w1 + w2)
        y_ref[0, :, pair * LANES:(pair + 1) * LANES] = y.astype(y_ref.dtype)


def _attention(qkv1, qkv2, qkv3, *, ta=2048):
    bsz, s, _ = qkv1.shape
    nt = s // ta
    assert s % ta == 0 and ta % (ATTN_QB * DILATIONS[2]) == 0 and (ta // ATTN_QB) % ATTN_BLOCKS_PER_ITER == 0
    qkv1 = qkv1.reshape(bsz, 1, s, QKV_WIDTH)
    hb = ATTN_HALF
    specs = []
    for dil in DILATIONS:
        n = ta // dil
        last = s // dil // hb - 1
        specs += [
            pl.BlockSpec((1, dil, n, QKV_WIDTH), lambda b, i: (b, 0, i, 0)),
            pl.BlockSpec((1, dil, hb, QKV_WIDTH), lambda b, i, n=n: (b, 0, jnp.maximum(i * (n // hb) - 1, 0), 0)),
            pl.BlockSpec((1, dil, hb, QKV_WIDTH),
                         lambda b, i, n=n, last=last: (b, 0, jnp.minimum((i + 1) * (n // hb), last), 0)),
        ]
    bias = _attn_bias_table()
    kvw = 2 * PATTERN_WIDTH
    return pl.pallas_call(
        functools.partial(_attn_kernel, ta=ta, nt=nt, seq=s),
        grid=(bsz, nt),
        in_specs=[pl.BlockSpec(bias.shape, lambda b, i: (0, 0, 0))] + specs,
        out_specs=pl.BlockSpec((1, ta, PATTERN_WIDTH), lambda b, i: (b, i, 0)),
        out_shape=jax.ShapeDtypeStruct((bsz, s, PATTERN_WIDTH), jnp.bfloat16),
        scratch_shapes=[pltpu.VMEM((dil, ta // dil + 2 * hb, kvw), jnp.bfloat16) for dil in DILATIONS] + [
            pltpu.VMEM((len(DILATIONS), PATTERN_WIDTH // LANES, ta, LANES), jnp.float32),
            pltpu.VMEM((len(DILATIONS), PATTERN_WIDTH // LANES, ta, LANES), jnp.float32),
        ],
        compiler_params=pltpu.CompilerParams(
            dimension_semantics=("arbitrary", "arbitrary"), vmem_limit_bytes=V7X_VMEM_LIMIT),
        name="dilated_attn",
    )(bias, qkv1, qkv1, qkv1, qkv2, qkv2, qkv2, qkv3, qkv3, qkv3)


def _layer_norm(v, g, b):
    mu = jnp.mean(v, axis=-1, keepdims=True)
    c = v - mu
    var = jnp.mean(c * c, axis=-1, keepdims=True)
    return c * lax.rsqrt(var + NORM_EPS) * g + b


def _merge_mlp_kernel(x_ref, yssd_ref, yatt_ref, gate_ref, bgate_ref, wps_ref, wpa_ref, wout_ref,
                      ln1g_ref, ln1b_ref, wup_ref, wdown_ref, ln2g_ref, ln2b_ref, o_ref, *, ffc):
    f32, bf16 = jnp.float32, jnp.bfloat16
    tm = x_ref.shape[1]

    def mixed(rs):
        ys = jnp.dot(yssd_ref[0, rs, :], wps_ref[...], preferred_element_type=f32)
        ya = jnp.dot(yatt_ref[0, rs, :], wpa_ref[...], preferred_element_type=f32)
        g_ssd = _sigmoid(gate_ref[0, rs, 0:D_MODEL].astype(f32) + bgate_ref[:, 0:D_MODEL])
        g_att = _sigmoid(gate_ref[0, rs, D_MODEL:2 * D_MODEL].astype(f32) + bgate_ref[:, D_MODEL:2 * D_MODEL])
        return jnp.dot((g_ssd * ys + g_att * ya).astype(bf16), wout_ref[...], preferred_element_type=f32)

    def norm1(rs, mix):
        return _layer_norm(ALPHA * x_ref[0, rs, :] + mix, ln1g_ref[...], ln1b_ref[...])

    def mlp(h):
        hb = h.astype(bf16)
        f = None
        for c in range(0, D_FF, ffc):
            up = jnp.maximum(jnp.dot(hb, wup_ref[:, c:c + ffc], preferred_element_type=f32), 0.0)
            part = jnp.dot((up * up).astype(bf16), wdown_ref[c:c + ffc, :], preferred_element_type=f32)
            f = part if f is None else f + part
        return f

    def norm2(rs, h, f):
        o_ref[0, rs, :] = _layer_norm(ALPHA * h + f, ln2g_ref[...], ln2b_ref[...]).astype(o_ref.dtype)

    ra, rb = slice(0, tm // 2), slice(tm // 2, tm)
    mix_a = mixed(ra)
    h_a = norm1(ra, mix_a)
    mix_b = mixed(rb)
    h_b = norm1(rb, mix_b)
    f_a = mlp(h_a)
    norm2(ra, h_a, f_a)
    f_b = mlp(h_b)
    norm2(rb, h_b, f_b)


def _merge_mlp(x, y_ssd, y_att, gate, b_gate, w_proj_ssd, w_proj_attn, w_out, ln1_g, ln1_b, w_up, w_down,
               ln2_g, ln2_b, *, tm=512, ffc=1024):
    bsz, s, d = x.shape
    assert s % tm == 0 and D_FF % ffc == 0
    bf16, f32 = jnp.bfloat16, jnp.float32
    row = lambda v: v.astype(f32).reshape(1, -1)
    tok = lambda width: pl.BlockSpec((1, tm, width), lambda b, i: (b, i, 0))
    resident = lambda shape: pl.BlockSpec(shape, lambda b, i: (0,) * len(shape), pipeline_mode=pl.Buffered(1))
    operands = [
        (x, tok(d)), (y_ssd, tok(D_INNER)), (y_att, tok(PATTERN_WIDTH)), (gate, tok(2 * D_MODEL)),
        (row(b_gate), None), (w_proj_ssd.astype(bf16), None), (w_proj_attn.astype(bf16), None),
        (w_out.astype(bf16), None), (row(ln1_g), None), (row(ln1_b), None),
        (w_up.astype(bf16), None), (w_down.astype(bf16), None), (row(ln2_g), None), (row(ln2_b), None),
    ]
    args = [a for a, _ in operands]
    in_specs = [spec if spec is not None else resident(a.shape) for a, spec in operands]
    return pl.pallas_call(
        functools.partial(_merge_mlp_kernel, ffc=ffc),
        grid=(bsz, s // tm),
        in_specs=in_specs,
        out_specs=tok(d),
        out_shape=jax.ShapeDtypeStruct((bsz, s, d), x.dtype),
        compiler_params=pltpu.CompilerParams(
            dimension_semantics=("arbitrary", "arbitrary"), vmem_limit_bytes=V7X_VMEM_LIMIT),
        name="merge_mlp",
    )(*args)


def kernel(x, w_in, b_gate, conv_w, conv_b, dt_bias_f, dt_bias_b, a_log_f, a_log_b, d_skip, ssd_norm_w,
           w_proj_ssd, w_proj_attn, w_out, ln1_g, ln1_b, w_up, w_down, ln2_g, ln2_b):
    w_packed, wdt_t = _pack_w_in(w_in)
    z, xc, qkv1, qkv2, qkv3, gate, dtt = _in_proj(x, w_packed, wdt_t, conv_w, conv_b)
    y_ssd = _ssd(xc, dtt, z, dt_bias_f, dt_bias_b, a_log_f, a_log_b, d_skip, ssd_norm_w)
    y_att = _attention(qkv1, qkv2, qkv3)
    return _merge_mlp(x, y_ssd, y_att, gate, b_gate, w_proj_ssd, w_proj_attn, w_out, ln1_g, ln1_b,
                      w_up, w_down, ln2_g, ln2_b)
```
